```python
import jax, jax.numpy as jnp
from jax import lax
import numpy as np

D_MODEL = 1024
BATCH = 16
SEQ = 2048
DEPTH = 1
DEC_BATCH = 32
DEC_SEQ = 64
PAST_LEN = 1024

CHUNK = 64
N_META = 16
HEAD_DIM = 64
D_ATT = D_MODEL // 2
D_CONV = D_MODEL - D_ATT
N_HEADS = D_ATT // HEAD_DIM
N_KV = 2
GROUP = N_HEADS // N_KV
KV_DIM = N_KV * HEAD_DIM
WINDOW = 128
BAND_CHUNKS = -(-WINDOW // CHUNK) + 1
CONV_W = 31
RMS_EPS = 1e-6
LN_EPS = 1e-5
ATTN_SCALE = HEAD_DIM ** -0.5
SPLIT_IDX = [D_ATT, D_ATT + KV_DIM, D_ATT + 2 * KV_DIM, 2 * D_ATT + 2 * KV_DIM,
             2 * D_ATT + 2 * KV_DIM + 2 * D_CONV]
D_IN = 2 * D_ATT + 2 * KV_DIM + 3 * D_CONV

kernel_name = "hymba_swa_sink_conformer_conv_stream_step"


def _rms(x, g):
    xf = x.astype(jnp.float32)
    y = xf * lax.rsqrt(jnp.mean(xf * xf, axis=-1, keepdims=True) + RMS_EPS)
    return (y * g.astype(jnp.float32)).astype(x.dtype)


def _layernorm(x, g, b):
    xf = x.astype(jnp.float32)
    mu = jnp.mean(xf, axis=-1, keepdims=True)
    xc = xf - mu
    y = xc * lax.rsqrt(jnp.mean(xc * xc, axis=-1, keepdims=True) + LN_EPS)
    return (y * g.astype(jnp.float32) + b.astype(jnp.float32)).astype(x.dtype)


def _alibi_slopes():
    h = jnp.arange(1, N_HEADS + 1, dtype=jnp.float32)
    return (2.0 ** (-8.0 * h / N_HEADS)).reshape(N_KV, GROUP)


def _split(z):
    lead = z.shape[:-1]
    q, k, v, ga, glu, gb = jnp.split(z, SPLIT_IDX, axis=-1)
    q = q.reshape(lead + (N_KV, GROUP, HEAD_DIM))
    k = k.reshape(lead + (N_KV, HEAD_DIM))
    v = v.reshape(lead + (N_KV, HEAD_DIM))
    a, b = jnp.split(glu, 2, axis=-1)
    u = a * jax.nn.sigmoid(b)
    return q, k, v, ga, u, gb


def _sink_attention(q, k, v, bias, sinks):
    s = jnp.einsum('bnqkgd,bnjkd->bnkgqj', q, k).astype(jnp.float32) * ATTN_SCALE + bias
    sink = sinks.astype(jnp.float32).reshape(N_KV, GROUP, 1, 1)
    m = jnp.maximum(jnp.max(s, axis=-1, keepdims=True), sink)
    e = jnp.exp(s - m)
    p = e / (jnp.sum(e, axis=-1, keepdims=True) + jnp.exp(sink - m))
    return jnp.einsum('bnkgqj,bnjkd->bnqkgd', p.astype(v.dtype), v)


def _prompt_attention(q, k, v, mk, mv, sinks):
    b, s = q.shape[0], q.shape[1]
    nc = s // CHUNK
    qc = q.reshape(b, nc, CHUNK, N_KV, GROUP, HEAD_DIM)

    def band(t, mt):
        tc = t.reshape(b, nc, CHUNK, N_KV, HEAD_DIM)
        tp = jnp.concatenate(
            [jnp.zeros((b, BAND_CHUNKS - 1, CHUNK, N_KV, HEAD_DIM), t.dtype), tc], axis=1)
        rows = [tp[:, j:j + nc] for j in range(BAND_CHUNKS)]
        meta = jnp.broadcast_to(mt, (b, nc, N_META, N_KV, HEAD_DIM))
        return jnp.concatenate([meta] + rows, axis=2)

    kb = band(k, mk)
    vb = band(v, mv)
    span = BAND_CHUNKS * CHUNK
    qi = jnp.arange(CHUNK)
    kj = jnp.arange(span)
    dist = jnp.abs(qi[:, None] + (BAND_CHUNKS - 1) * CHUNK - kj[None, :]).astype(jnp.float32)
    alibi = -_alibi_slopes()[:, :, None, None] * dist
    key_chunk = jnp.arange(nc)[:, None] - (BAND_CHUNKS - 1) + kj[None, :] // CHUNK
    band_bias = jnp.where((key_chunk >= 0)[:, None, None, None, :], alibi, -jnp.inf)
    bias = jnp.concatenate(
        [jnp.zeros((nc, N_KV, GROUP, CHUNK, N_META), jnp.float32), band_bias], axis=-1)
    out = _sink_attention(qc, kb, vb, bias[None], sinks)
    return out.reshape(b, s, D_ATT)


def _sample_attention(q, k, v, mk, mv, ck, cv, sinks):
    b, t = q.shape[0], q.shape[1]
    rows = ck.shape[1]

    def keys(new, cache, mt):
        meta = jnp.broadcast_to(mt, (b, N_META, N_KV, HEAD_DIM))
        return jnp.concatenate([meta, cache, new], axis=1)[:, None]

    qpos = jnp.arange(t)
    kpos = jnp.arange(rows + t) - rows
    dist = jnp.abs(qpos[:, None] - kpos[None, :]).astype(jnp.float32)
    alibi = -_alibi_slopes()[:, :, None, None] * dist
    bias = jnp.concatenate(
        [jnp.zeros((N_KV, GROUP, t, N_META), jnp.float32), alibi], axis=-1)
    out = _sink_attention(q[:, None], keys(k, ck, mk), keys(v, cv, mv), bias[None, None], sinks)
    return out.reshape(b, t, D_ATT)


def _dwconv(x, w):
    return lax.conv_general_dilated(
        x, w[:, None, :].astype(x.dtype), window_strides=(1,), padding='VALID',
        dimension_numbers=('NWC', 'WIO', 'NWC'), feature_group_count=x.shape[-1])


def _conv_branch(u, left, conv_w, ln_g, ln_b, w_pw):
    h = _dwconv(jnp.concatenate([left, u], axis=1), conv_w)
    h = jax.nn.silu(_layernorm(h, ln_g, ln_b))
    return h @ w_pw


def _merge(att, ga, conv, gb, g_att, g_conv, w_out, g_post):
    y = jnp.concatenate([_rms(att, g_att) * jax.nn.silu(ga),
                         _rms(conv, g_conv) * jax.nn.silu(gb)], axis=-1) @ w_out
    return _rms(y, g_post)


def setup_inputs(seed: int = 0) -> dict:
    key = jax.random.key(seed)
    ks = jax.random.split(key, 18)
    rows = min(WINDOW, PAST_LEN)
    f32 = jnp.float32
    nrm = lambda k, shp: jax.random.normal(k, shp, f32)
    return {
        "x_prompt": nrm(ks[0], (BATCH, SEQ, D_MODEL)),
        "x_sample": nrm(ks[1], (DEC_BATCH, DEC_SEQ, D_MODEL)),
        "cache_k": nrm(ks[2], (DEPTH, DEC_BATCH, rows, N_KV, HEAD_DIM)),
        "cache_v": nrm(ks[3], (DEPTH, DEC_BATCH, rows, N_KV, HEAD_DIM)),
        "state_conv": 0.5 * nrm(ks[4], (DEPTH, DEC_BATCH, CONV_W - 1, D_CONV)),
        "meta_tokens": nrm(ks[5], (N_META, D_MODEL)),
        "g_pre": 1.0 + 0.05 * nrm(ks[6], (DEPTH, D_MODEL)),
        "w_in": nrm(ks[7], (DEPTH, D_MODEL, D_IN)) * D_MODEL ** -0.5,
        "sinks": 0.5 * nrm(ks[8], (DEPTH, N_HEADS)),
        "g_att": 1.0 + 0.05 * nrm(ks[9], (DEPTH, D_ATT)),
        "conv_w": nrm(ks[10], (DEPTH, CONV_W, D_CONV)) * CONV_W ** -0.5,
        "ln_g": 1.0 + 0.05 * nrm(ks[11], (DEPTH, D_CONV)),
        "ln_b": 0.02 * nrm(ks[12], (DEPTH, D_CONV)),
        "w_pw": nrm(ks[13], (DEPTH, D_CONV, D_CONV)) * D_CONV ** -0.5,
        "g_conv": 1.0 + 0.05 * nrm(ks[14], (DEPTH, D_CONV)),
        "w_out": nrm(ks[15], (DEPTH, D_ATT + D_CONV, D_MODEL)) * (D_ATT + D_CONV) ** -0.5,
        "g_post": 1.0 + 0.05 * nrm(ks[16], (DEPTH, D_MODEL)),
    }


def reference(x_prompt, x_sample, cache_k, cache_v, state_conv, meta_tokens, g_pre, w_in,
              sinks, g_att, conv_w, ln_g, ln_b, w_pw, g_conv, w_out, g_post):
    h_p = x_prompt
    h_s = x_sample
    mh = meta_tokens
    b_p = x_prompt.shape[0]
    nk_p, nv_p, nc_p, nk_s, nv_s, nc_s = [], [], [], [], [], []
    for l in range(DEPTH):
        mq, mk, mv, mga, mu, mgb = _split(_rms(mh, g_pre[l]) @ w_in[l])
        meta_left = jnp.concatenate(
            [jnp.zeros((max(CONV_W - 1 - N_META, 0), D_CONV), mu.dtype), mu[-(CONV_W - 1):]], axis=0)

        q, k, v, ga, u, gb = _split(_rms(h_p, g_pre[l]) @ w_in[l])
        att = _prompt_attention(q, k, v, mk, mv, sinks[l])
        conv = _conv_branch(u, jnp.broadcast_to(meta_left, (b_p, CONV_W - 1, D_CONV)),
                            conv_w[l], ln_g[l], ln_b[l], w_pw[l])
        h_p = h_p + _merge(att, ga, conv, gb, g_att[l], g_conv[l], w_out[l], g_post[l])
        nk_p.append(k[:, -WINDOW:])
        nv_p.append(v[:, -WINDOW:])
        nc_p.append(u[:, -(CONV_W - 1):])

        rows = cache_k.shape[2]
        qs, ks_, vs, gas, us, gbs = _split(_rms(h_s, g_pre[l]) @ w_in[l])
        att_s = _sample_attention(qs, ks_, vs, mk, mv, cache_k[l], cache_v[l], sinks[l])
        conv_s = _conv_branch(us, state_conv[l], conv_w[l], ln_g[l], ln_b[l], w_pw[l])
        h_s = h_s + _merge(att_s, gas, conv_s, gbs, g_att[l], g_conv[l], w_out[l], g_post[l])
        nk_s.append(jnp.concatenate([cache_k[l], ks_], axis=1)[:, -rows:])
        nv_s.append(jnp.concatenate([cache_v[l], vs], axis=1)[:, -rows:])
        nc_s.append(jnp.concatenate([state_conv[l], us], axis=1)[:, -(CONV_W - 1):])

        if l + 1 < DEPTH:
            m_att = _sink_attention(mq[None, None], mk[None, None], mv[None, None],
                                    jnp.zeros((1, 1, N_KV, GROUP, N_META, N_META), jnp.float32),
                                    sinks[l])[0, 0].reshape(N_META, D_ATT)
            m_conv = _conv_branch(mu[None], jnp.zeros((1, CONV_W - 1, D_CONV), mu.dtype),
                                  conv_w[l], ln_g[l], ln_b[l], w_pw[l])[0]
            mh = mh + _merge(m_att, mga, m_conv, mgb, g_att[l], g_conv[l], w_out[l], g_post[l])

    new_k_prompt = jnp.stack(nk_p, axis=0)
    new_v_prompt = jnp.stack(nv_p, axis=0)
    new_conv_prompt = jnp.stack(nc_p, axis=0)
    new_k_sample = jnp.stack(nk_s, axis=0)
    new_v_sample = jnp.stack(nv_s, axis=0)
    new_conv_sample = jnp.stack(nc_s, axis=0)
    return (h_p, h_s, new_k_prompt, new_v_prompt, new_conv_prompt,
            new_k_sample, new_v_sample, new_conv_sample)
```

```python
import functools

import jax
import jax.numpy as jnp
import numpy as np
from jax import lax
from jax.experimental import pallas as pl
from jax.experimental.pallas import tpu as pltpu

D_MODEL = 1024
CHUNK = 64
N_META = 16
HEAD_DIM = 64
D_ATT = 512
D_CONV = 512
N_KV = 2
GROUP = 4
N_HEADS = 8
KV_DIM = N_KV * HEAD_DIM
WINDOW = 128
BAND = WINDOW + CHUNK
CONV_W = 31
LEFT = 32
RMS_EPS = 1e-6
LN_EPS = 1e-5
ATTN_SCALE = HEAD_DIM ** -0.5
D_IN = 2 * D_ATT + 2 * KV_DIM + 3 * D_CONV

C_Q, C_K, C_V, C_GA, C_A, C_B, C_GB = 0, 512, 640, 768, 1280, 1792, 2304

KEYS = N_META + BAND
KPAD = 256
SINK_COL = KEYS

MM_ROWS = 256
CONV_ROWS = 32

VMEM_LIMIT_BYTES = 56 * 1024 * 1024


def _att_perm():
    n = np.arange(D_ATT)
    g, h, d = n // 128, (n // 64) % 2, n % 64
    return h * 256 + g * 64 + d


def _bias_table(sinks):
    r = np.arange(N_HEADS * CHUNK)
    g, h, q = r // 128, (r // 64) % 2, r % 64
    head = h * GROUP + g
    slope = 2.0 ** (-(head + 1.0))
    j = np.arange(BAND)
    dist = np.abs(q[:, None] + WINDOW - j[None, :]).astype(np.float32)
    alibi = jnp.asarray(-slope[:, None].astype(np.float32) * dist)
    sink = sinks.astype(jnp.float32)[head][:, None]
    return jnp.concatenate(
        [jnp.zeros((r.size, N_META), jnp.float32), alibi, sink,
         jnp.full((r.size, KPAD - KEYS - 1), -jnp.inf, jnp.float32)], axis=1)


def _rms_rows(v):
    return lax.rsqrt(jnp.mean(v * v, axis=-1, keepdims=True) + RMS_EPS)


def _silu(v):
    return v * jax.nn.sigmoid(v)


def _dot(a, b):
    return jnp.dot(a, b, preferred_element_type=jnp.float32)


def _meta_kernel(m_ref, gpre_ref, w_ref, mk_ref, mv_ref, left_ref):
    m = m_ref[...]
    xn = (m * _rms_rows(m) * gpre_ref[...]).astype(jnp.bfloat16)
    kv = _dot(xn, w_ref[:, C_K:C_GA])
    mk_ref[...] = kv[:, :KV_DIM]
    mv_ref[...] = kv[:, KV_DIM:]
    ab = _dot(xn, w_ref[:, C_A:C_GB])
    mu = ab[:, :D_CONV] * jax.nn.sigmoid(ab[:, D_CONV:])
    left_ref[...] = jnp.concatenate(
        [jnp.zeros((LEFT - N_META, D_CONV), jnp.float32), mu], axis=0)


def _layer_kernel(*refs, tile, seq_tiled):
    n_chunks = tile // CHUNK
    if seq_tiled:
        (x_ref, mk_ref, mv_ref, left_ref,
         gpre_ref, win_ref, bias_ref, gatt_ref, cw_ref, lng_ref, lnb_ref, wpw_ref,
         gconv_ref, wout_ref, gpost_ref,
         o_ref, nk_ref, nv_ref, nc_ref,
         xn_ref, q_ref, sga_ref, sgb_ref, kbuf, vbuf, ubuf, hn_ref, cat_ref) = refs
        band_stride, u_stride = CHUNK, CHUNK
    else:
        (x_ref, mk_ref, mv_ref, left_ref, ck_ref, cv_ref,
         gpre_ref, win_ref, bias_ref, gatt_ref, cw_ref, lng_ref, lnb_ref, wpw_ref,
         gconv_ref, wout_ref, gpost_ref,
         o_ref, nk_ref, nv_ref, nc_ref,
         xn_ref, q_ref, sga_ref, sgb_ref, kbuf, vbuf, ubuf, hn_ref, cat_ref) = refs
        band_stride, u_stride = BAND, LEFT + CHUNK

    first = pl.program_id(1) == 0

    if seq_tiled:
        @pl.when(first)
        def _():
            kbuf[0:WINDOW, :] = jnp.zeros((WINDOW, KV_DIM), jnp.bfloat16)
            vbuf[0:WINDOW, :] = jnp.zeros((WINDOW, KV_DIM), jnp.bfloat16)
            ubuf[0:LEFT, :] = left_ref[0]
    else:
        for j in range(n_chunks):
            kbuf[j * BAND:j * BAND + WINDOW, :] = ck_ref[j].astype(jnp.bfloat16)
            vbuf[j * BAND:j * BAND + WINDOW, :] = cv_ref[j].astype(jnp.bfloat16)
            ubuf[j * u_stride:j * u_stride + LEFT, :] = left_ref[j]

    gpre = gpre_ref[...]
    for c in range(n_chunks):
        rows = slice(c * CHUNK, (c + 1) * CHUNK)
        xc = x_ref[rows, :]
        xn_ref[rows, :] = (xc * _rms_rows(xc) * gpre).astype(jnp.bfloat16)

    for rb in range(tile // MM_ROWS):
        rows = slice(rb * MM_ROWS, (rb + 1) * MM_ROWS)
        xb = xn_ref[rows, :]
        q_ref[rows, :] = _dot(xb, win_ref[:, C_Q:C_K]) * ATTN_SCALE
        kv = _dot(xb, win_ref[:, C_K:C_GA])
        for cc in range(MM_ROWS // CHUNK):
            c = rb * (MM_ROWS // CHUNK) + cc
            piece = kv[cc * CHUNK:(cc + 1) * CHUNK, :]
            dst = WINDOW + c * band_stride
            kbuf[dst:dst + CHUNK, :] = piece[:, :KV_DIM].astype(jnp.bfloat16)
            vbuf[dst:dst + CHUNK, :] = piece[:, KV_DIM:].astype(jnp.bfloat16)
            if not seq_tiled:
                nk_ref[c, 0:WINDOW - CHUNK, :] = ck_ref[c, CHUNK:WINDOW, :]
                nv_ref[c, 0:WINDOW - CHUNK, :] = cv_ref[c, CHUNK:WINDOW, :]
                nk_ref[c, WINDOW - CHUNK:WINDOW, :] = piece[:, :KV_DIM]
                nv_ref[c, WINDOW - CHUNK:WINDOW, :] = piece[:, KV_DIM:]
        if seq_tiled and rb == tile // MM_ROWS - 1:
            nk_ref[0] = kv[MM_ROWS - WINDOW:, :KV_DIM]
            nv_ref[0] = kv[MM_ROWS - WINDOW:, KV_DIM:]
        sga_ref[rows, :] = _silu(_dot(xb, win_ref[:, C_GA:C_A]))
        ab = _dot(xb, win_ref[:, C_A:C_GB])
        u = ab[:, :D_CONV] * jax.nn.sigmoid(ab[:, D_CONV:])
        for cc in range(MM_ROWS // CHUNK):
            c = rb * (MM_ROWS // CHUNK) + cc
            dst = LEFT + c * u_stride
            ubuf[dst:dst + CHUNK, :] = u[cc * CHUNK:(cc + 1) * CHUNK, :]
        sgb_ref[rows, :] = _silu(_dot(xb, win_ref[:, C_GB:D_IN]))

    lane = lax.broadcasted_iota(jnp.int32, (CHUNK, KV_DIM), 1)
    lo_half = lane < HEAD_DIM
    col = lax.broadcasted_iota(jnp.int32, (N_HEADS * CHUNK, KPAD), 1)
    mk = mk_ref[...].astype(jnp.bfloat16)
    mv = mv_ref[...].astype(jnp.bfloat16)
    kv_pad = jnp.zeros((KPAD - KEYS, KV_DIM), jnp.bfloat16)
    gatt = gatt_ref[...]
    for c in range(n_chunks):
        rows = slice(c * CHUNK, (c + 1) * CHUNK)
        qc = q_ref[rows, :]
        blocks = []
        for g in range(GROUP):
            qg = qc[:, g * KV_DIM:(g + 1) * KV_DIM]
            blocks.append(jnp.where(lo_half, qg, 0.0))
            blocks.append(jnp.where(lo_half, 0.0, qg))
        qs = jnp.concatenate(blocks, axis=0).astype(jnp.bfloat16)
        b0 = c * band_stride
        kb = jnp.concatenate([mk, kbuf[b0:b0 + BAND, :], kv_pad], axis=0)
        vb = jnp.concatenate([mv, vbuf[b0:b0 + BAND, :], kv_pad], axis=0)
        s = lax.dot_general(qs, kb, (((1,), (1,)), ((), ())),
                            preferred_element_type=jnp.float32)
        s = s + bias_ref[...]
        if seq_tiled and c < WINDOW // CHUNK:
            n_dead = jnp.where(first, WINDOW - c * CHUNK, 0)
            dead = (col >= N_META) & (col < N_META + n_dead)
            s = jnp.where(dead, -jnp.inf, s)
        m = jnp.max(s, axis=-1, keepdims=True)
        e = jnp.exp(s - m)
        p = e * (1.0 / jnp.sum(e, axis=-1, keepdims=True))
        o = _dot(p.astype(jnp.bfloat16), vb)
        att = jnp.concatenate(
            [jnp.where(lo_half, o[(2 * g) * CHUNK:(2 * g + 1) * CHUNK, :],
                       o[(2 * g + 1) * CHUNK:(2 * g + 2) * CHUNK, :])
             for g in range(GROUP)], axis=1)
        an = att * _rms_rows(att) * gatt * sga_ref[rows, :]
        cat_ref[rows, 0:D_ATT] = an.astype(jnp.bfloat16)

    lng, lnb = lng_ref[...], lnb_ref[...]
    for c in range(n_chunks):
        for r in range(CHUNK // CONV_ROWS):
            src = c * u_stride + r * CONV_ROWS + (LEFT - (CONV_W - 1))
            acc = jnp.zeros((CONV_ROWS, D_CONV), jnp.float32)
            for k in range(CONV_W):
                acc = acc + ubuf[src + k:src + k + CONV_ROWS, :] * cw_ref[k:k + 1, :]
            mu = jnp.mean(acc, axis=-1, keepdims=True)
            hc = acc - mu
            ln = hc * lax.rsqrt(jnp.mean(hc * hc, axis=-1, keepdims=True) + LN_EPS) * lng + lnb
            dst = c * CHUNK + r * CONV_ROWS
            hn_ref[dst:dst + CONV_ROWS, :] = _silu(ln).astype(jnp.bfloat16)

    gconv = gconv_ref[...]
    for rb in range(tile // MM_ROWS):
        rows = slice(rb * MM_ROWS, (rb + 1) * MM_ROWS)
        cv = _dot(hn_ref[rows, :], wpw_ref[...])
        cn = cv * _rms_rows(cv) * gconv * sgb_ref[rows, :]
        cat_ref[rows, D_ATT:D_MODEL] = cn.astype(jnp.bfloat16)
        o_ref[rows, :] = _dot(cat_ref[rows, :], wout_ref[...])

    gpost = gpost_ref[...]
    for c in range(n_chunks):
        rows = slice(c * CHUNK, (c + 1) * CHUNK)
        y = o_ref[rows, :]
        o_ref[rows, :] = x_ref[rows, :] + y * _rms_rows(y) * gpost

    if seq_tiled:
        nc_ref[0] = ubuf[tile:tile + LEFT, :]
        kbuf[0:WINDOW, :] = kbuf[tile:tile + WINDOW, :]
        vbuf[0:WINDOW, :] = vbuf[tile:tile + WINDOW, :]
        ubuf[0:LEFT, :] = ubuf[tile:tile + LEFT, :]
    else:
        for j in range(n_chunks):
            nc_ref[j] = ubuf[j * u_stride + CHUNK:j * u_stride + CHUNK + LEFT, :]


def _const_spec(shape):
    return pl.BlockSpec(shape, lambda i, s: (0,) * len(shape))


def _run_layer(x2d, mk, mv, left, caches, weights, *, tile, seq_tiled, n_streams, seq_len):
    n_chunks = tile // CHUNK
    if seq_tiled:
        steps = seq_len // tile
        grid = (n_streams, steps)
        x_map = lambda i, s: (i * steps + s, 0)
        st_block, st_map = 1, (lambda i, s: (i, 0, 0))
        left_spec = pl.BlockSpec((1, LEFT, D_CONV), lambda i, s: (0, 0, 0))
        cache_specs = []
        kv_rows = WINDOW + tile
        u_rows = LEFT + tile
    else:
        assert seq_len == CHUNK
        grid = (n_streams // n_chunks, 1)
        x_map = lambda i, s: (i, 0)
        st_block, st_map = n_chunks, (lambda i, s: (i, 0, 0))
        left_spec = pl.BlockSpec((n_chunks, LEFT, D_CONV), st_map)
        cache_specs = [pl.BlockSpec((n_chunks, WINDOW, KV_DIM), st_map)] * 2
        kv_rows = n_chunks * BAND
        u_rows = n_chunks * (LEFT + CHUNK)

    in_specs = ([pl.BlockSpec((tile, D_MODEL), x_map),
                 _const_spec((N_META, KV_DIM)), _const_spec((N_META, KV_DIM)), left_spec]
                + cache_specs + [_const_spec(w.shape) for w in weights])
    out_shape = (jax.ShapeDtypeStruct(x2d.shape, jnp.float32),
                 jax.ShapeDtypeStruct((n_streams, WINDOW, KV_DIM), jnp.float32),
                 jax.ShapeDtypeStruct((n_streams, WINDOW, KV_DIM), jnp.float32),
                 jax.ShapeDtypeStruct((n_streams, LEFT, D_CONV), jnp.float32))
    out_specs = (pl.BlockSpec((tile, D_MODEL), x_map),
                 pl.BlockSpec((st_block, WINDOW, KV_DIM), st_map),
                 pl.BlockSpec((st_block, WINDOW, KV_DIM), st_map),
                 pl.BlockSpec((st_block, LEFT, D_CONV), st_map))
    scratch = [pltpu.VMEM((tile, D_MODEL), jnp.bfloat16),
               pltpu.VMEM((tile, D_ATT), jnp.float32),
               pltpu.VMEM((tile, D_ATT), jnp.float32),
               pltpu.VMEM((tile, D_CONV), jnp.float32),
               pltpu.VMEM((kv_rows, KV_DIM), jnp.bfloat16),
               pltpu.VMEM((kv_rows, KV_DIM), jnp.bfloat16),
               pltpu.VMEM((u_rows, D_CONV), jnp.float32),
               pltpu.VMEM((tile, D_CONV), jnp.bfloat16),
               pltpu.VMEM((tile, D_MODEL), jnp.bfloat16)]
    return pl.pallas_call(
        functools.partial(_layer_kernel, tile=tile, seq_tiled=seq_tiled),
        grid=grid, in_specs=in_specs, out_specs=out_specs, out_shape=out_shape,
        scratch_shapes=scratch,
        compiler_params=pltpu.CompilerParams(
            dimension_semantics=("arbitrary", "arbitrary"),
            vmem_limit_bytes=VMEM_LIMIT_BYTES),
        name="layer_prompt" if seq_tiled else "layer_sample",
    )(x2d, mk, mv, left, *caches, *weights)


def kernel(x_prompt, x_sample, cache_k, cache_v, state_conv, meta_tokens, g_pre, w_in, sinks,
           g_att, conv_w, ln_g, ln_b, w_pw, g_conv, w_out, g_post):
    assert w_in.shape == (1, D_MODEL, D_IN) and cache_k.shape[2] == WINDOW
    b_p, s_p, _ = x_prompt.shape
    b_s, s_s, _ = x_sample.shape
    perm = _att_perm()
    row = lambda v: v[0][None, :].astype(jnp.float32)

    w = w_in[0]
    w_in_p = jnp.concatenate(
        [w[:, C_Q:C_K][:, perm], w[:, C_K:C_GA], w[:, C_GA:C_A][:, perm], w[:, C_A:]],
        axis=1).astype(jnp.bfloat16)
    w_out_p = jnp.concatenate([w_out[0][:D_ATT][perm], w_out[0][D_ATT:]], axis=0).astype(jnp.bfloat16)
    weights = (row(g_pre), w_in_p, _bias_table(sinks[0]), row(g_att)[:, perm], conv_w[0],
               row(ln_g), row(ln_b), w_pw[0].astype(jnp.bfloat16), row(g_conv), w_out_p, row(g_post))

    mk, mv, meta_left = pl.pallas_call(
        _meta_kernel,
        out_shape=(jax.ShapeDtypeStruct((N_META, KV_DIM), jnp.float32),
                   jax.ShapeDtypeStruct((N_META, KV_DIM), jnp.float32),
                   jax.ShapeDtypeStruct((LEFT, D_CONV), jnp.float32)),
        compiler_params=pltpu.CompilerParams(vmem_limit_bytes=VMEM_LIMIT_BYTES),
        name="meta_tokens",
    )(meta_tokens, row(g_pre), w_in_p)

    y_p, nk_p, nv_p, nc_p = _run_layer(
        x_prompt.reshape(b_p * s_p, D_MODEL), mk, mv, meta_left[None], (), weights,
        tile=256, seq_tiled=True, n_streams=b_p, seq_len=s_p)

    ck = cache_k[0].reshape(b_s, WINDOW, KV_DIM)
    cv = cache_v[0].reshape(b_s, WINDOW, KV_DIM)
    left_s = jnp.pad(state_conv[0], ((0, 0), (LEFT - (CONV_W - 1), 0), (0, 0)))
    y_s, nk_s, nv_s, nc_s = _run_layer(
        x_sample.reshape(b_s * s_s, D_MODEL), mk, mv, left_s, (ck, cv), weights,
        tile=256, seq_tiled=False, n_streams=b_s, seq_len=s_s)

    kv5 = lambda a, n: a.reshape(1, n, WINDOW, N_KV, HEAD_DIM)
    tail = lambda a: a[None, :, LEFT - (CONV_W - 1):, :]
    return (y_p.reshape(b_p, s_p, D_MODEL), y_s.reshape(b_s, s_s, D_MODEL),
            kv5(nk_p, b_p), kv5(nv_p, b_p), tail(nc_p),
            kv5(nk_s, b_s), kv5(nv_s, b_s), tail(nc_s))
```

```python
import functools

import jax
import jax.numpy as jnp
import numpy as np
from jax import lax
from jax.experimental import pallas as pl
from jax.experimental.pallas import tpu as pltpu

D_MODEL = 1024
CHUNK = 64
N_META = 16
HEAD_DIM = 64
D_ATT = 512
D_CONV = 512
N_KV = 2
GROUP = 4
N_HEADS = 8
KV_DIM = N_KV * HEAD_DIM
WINDOW = 128
BAND = WINDOW + CHUNK
CONV_W = 31
LEFT = 32
RMS_EPS = 1e-6
LN_EPS = 1e-5
ATTN_SCALE = HEAD_DIM ** -0.5
D_IN = 2 * D_ATT + 2 * KV_DIM + 3 * D_CONV

C_Q, C_K, C_V, C_GA, C_A, C_B, C_GB = 0, 512, 640, 768, 1280, 1792, 2304

KEYS = N_META + BAND
KPAD = 256
SINK_COL = KEYS

MM_ROWS = 256
SUBLANES, LANES = 8, 128
TILE_ROWS = 256

VMEM_LIMIT_BYTES = 56 * 1024 * 1024


def _att_perm():
    n = np.arange(D_ATT)
    g, h, d = n // 128, (n // 64) % 2, n % 64
    return h * 256 + g * 64 + d


def _bias_table(sinks):
    r = np.arange(N_HEADS * CHUNK)
    g, h, q = r // 128, (r // 64) % 2, r % 64
    head = h * GROUP + g
    slope = 2.0 ** (-(head + 1.0))
    j = np.arange(BAND)
    dist = np.abs(q[:, None] + WINDOW - j[None, :]).astype(np.float32)
    alibi = jnp.asarray(-slope[:, None].astype(np.float32) * dist)
    sink = sinks.astype(jnp.float32)[head][:, None]
    return jnp.concatenate(
        [jnp.zeros((r.size, N_META), jnp.float32), alibi, sink,
         jnp.full((r.size, KPAD - KEYS - 1), -jnp.inf, jnp.float32)], axis=1)


def _rms_rows(v):
    return lax.rsqrt(jnp.mean(v * v, axis=-1, keepdims=True) + RMS_EPS)


def _silu(v):
    return v * jax.nn.sigmoid(v)


def _dot(a, b):
    return jnp.dot(a, b, preferred_element_type=jnp.float32)


def _dwconv_block(ubuf, cw_ref, src, n_rows, lane0):
    pad = LEFT - (CONV_W - 1)
    n_groups = n_rows // SUBLANES
    lanes = slice(lane0, lane0 + LANES)
    xg = [ubuf[src + SUBLANES * i:src + SUBLANES * (i + 1), lanes]
          for i in range(n_groups + LEFT // SUBLANES)]
    sublane = lax.broadcasted_iota(jnp.int32, (SUBLANES, LANES), 0)
    acc = None
    for r in range(SUBLANES):
        offs = [o for o in range(r, LEFT + 1, SUBLANES) if 0 <= o - pad < CONV_W]
        taps = [(o // SUBLANES, cw_ref[(o - pad) * SUBLANES:(o - pad + 1) * SUBLANES, lanes])
                for o in offs]
        y = []
        for j in range(n_groups + (r > 0)):
            t = None
            for m, w in taps:
                term = xg[j + m] * w
                t = term if t is None else t + term
            y.append(t)
        if r == 0:
            acc = y
        else:
            rolled = [pltpu.roll(v, SUBLANES - r, 0) for v in y]
            keep = sublane < SUBLANES - r
            acc = [a + jnp.where(keep, rolled[j], rolled[j + 1]) for j, a in enumerate(acc)]
    return acc


def _meta_kernel(m_ref, gpre_ref, w_ref, mk_ref, mv_ref, left_ref):
    m = m_ref[...]
    xn = (m * _rms_rows(m) * gpre_ref[...]).astype(jnp.bfloat16)
    kv = _dot(xn, w_ref[:, C_K:C_GA])
    mk_ref[...] = kv[:, :KV_DIM]
    mv_ref[...] = kv[:, KV_DIM:]
    ab = _dot(xn, w_ref[:, C_A:C_GB])
    mu = ab[:, :D_CONV] * jax.nn.sigmoid(ab[:, D_CONV:])
    left_ref[...] = jnp.concatenate(
        [jnp.zeros((LEFT - N_META, D_CONV), jnp.float32), mu], axis=0)


def _layer_kernel(*refs, tile, seq_tiled):
    n_chunks = tile // CHUNK
    if seq_tiled:
        (x_ref, mk_ref, mv_ref, left_ref,
         gpre_ref, win_ref, bias_ref, gatt_ref, cw_ref, lng_ref, lnb_ref, wpw_ref,
         gconv_ref, wout_ref, gpost_ref,
         o_ref, nk_ref, nv_ref, nc_ref,
         xn_ref, q_ref, sga_ref, sgb_ref, kbuf, vbuf, ubuf, h_ref, hn_ref, cat_ref) = refs
        band_stride, u_stride = CHUNK, CHUNK
    else:
        (x_ref, mk_ref, mv_ref, left_ref, ck_ref, cv_ref,
         gpre_ref, win_ref, bias_ref, gatt_ref, cw_ref, lng_ref, lnb_ref, wpw_ref,
         gconv_ref, wout_ref, gpost_ref,
         o_ref, nk_ref, nv_ref, nc_ref,
         xn_ref, q_ref, sga_ref, sgb_ref, kbuf, vbuf, ubuf, h_ref, hn_ref, cat_ref) = refs
        band_stride, u_stride = BAND, LEFT + CHUNK

    first = pl.program_id(1) == 0

    if seq_tiled:
        @pl.when(first)
        def _():
            kbuf[0:WINDOW, :] = jnp.zeros((WINDOW, KV_DIM), jnp.bfloat16)
            vbuf[0:WINDOW, :] = jnp.zeros((WINDOW, KV_DIM), jnp.bfloat16)
            ubuf[0:LEFT, :] = left_ref[0]
    else:
        for j in range(n_chunks):
            kbuf[j * BAND:j * BAND + WINDOW, :] = ck_ref[j].astype(jnp.bfloat16)
            vbuf[j * BAND:j * BAND + WINDOW, :] = cv_ref[j].astype(jnp.bfloat16)
            ubuf[j * u_stride:j * u_stride + LEFT, :] = left_ref[j]

    gpre = gpre_ref[...]
    for c in range(n_chunks):
        rows = slice(c * CHUNK, (c + 1) * CHUNK)
        xc = x_ref[rows, :]
        xn_ref[rows, :] = (xc * _rms_rows(xc) * gpre).astype(jnp.bfloat16)

    for rb in range(tile // MM_ROWS):
        rows = slice(rb * MM_ROWS, (rb + 1) * MM_ROWS)
        xb = xn_ref[rows, :]
        q_ref[rows, :] = _dot(xb, win_ref[:, C_Q:C_K]) * ATTN_SCALE
        kv = _dot(xb, win_ref[:, C_K:C_GA])
        for cc in range(MM_ROWS // CHUNK):
            c = rb * (MM_ROWS // CHUNK) + cc
            piece = kv[cc * CHUNK:(cc + 1) * CHUNK, :]
            dst = WINDOW + c * band_stride
            kbuf[dst:dst + CHUNK, :] = piece[:, :KV_DIM].astype(jnp.bfloat16)
            vbuf[dst:dst + CHUNK, :] = piece[:, KV_DIM:].astype(jnp.bfloat16)
            if not seq_tiled:
                nk_ref[c, 0:WINDOW - CHUNK, :] = ck_ref[c, CHUNK:WINDOW, :]
                nv_ref[c, 0:WINDOW - CHUNK, :] = cv_ref[c, CHUNK:WINDOW, :]
                nk_ref[c, WINDOW - CHUNK:WINDOW, :] = piece[:, :KV_DIM]
                nv_ref[c, WINDOW - CHUNK:WINDOW, :] = piece[:, KV_DIM:]
        if seq_tiled and rb == tile // MM_ROWS - 1:
            nk_ref[0] = kv[MM_ROWS - WINDOW:, :KV_DIM]
            nv_ref[0] = kv[MM_ROWS - WINDOW:, KV_DIM:]
        sga_ref[rows, :] = _silu(_dot(xb, win_ref[:, C_GA:C_A]))
        ab = _dot(xb, win_ref[:, C_A:C_GB])
        u = ab[:, :D_CONV] * jax.nn.sigmoid(ab[:, D_CONV:])
        for cc in range(MM_ROWS // CHUNK):
            c = rb * (MM_ROWS // CHUNK) + cc
            dst = LEFT + c * u_stride
            ubuf[dst:dst + CHUNK, :] = u[cc * CHUNK:(cc + 1) * CHUNK, :]
        sgb_ref[rows, :] = _silu(_dot(xb, win_ref[:, C_GB:D_IN]))

    lane = lax.broadcasted_iota(jnp.int32, (CHUNK, KV_DIM), 1)
    lo_half = lane < HEAD_DIM
    col = lax.broadcasted_iota(jnp.int32, (N_HEADS * CHUNK, KPAD), 1)
    mk = mk_ref[...].astype(jnp.bfloat16)
    mv = mv_ref[...].astype(jnp.bfloat16)
    kv_pad = jnp.zeros((KPAD - KEYS, KV_DIM), jnp.bfloat16)
    gatt = gatt_ref[...]
    for c in range(n_chunks):
        rows = slice(c * CHUNK, (c + 1) * CHUNK)
        qc = q_ref[rows, :]
        blocks = []
        for g in range(GROUP):
            qg = qc[:, g * KV_DIM:(g + 1) * KV_DIM]
            blocks.append(jnp.where(lo_half, qg, 0.0))
            blocks.append(jnp.where(lo_half, 0.0, qg))
        qs = jnp.concatenate(blocks, axis=0).astype(jnp.bfloat16)
        b0 = c * band_stride
        kb = jnp.concatenate([mk, kbuf[b0:b0 + BAND, :], kv_pad], axis=0)
        vb = jnp.concatenate([mv, vbuf[b0:b0 + BAND, :], kv_pad], axis=0)
        s = lax.dot_general(qs, kb, (((1,), (1,)), ((), ())),
                            preferred_element_type=jnp.float32)
        s = s + bias_ref[...]
        if seq_tiled and c < WINDOW // CHUNK:
            n_dead = jnp.where(first, WINDOW - c * CHUNK, 0)
            dead = (col >= N_META) & (col < N_META + n_dead)
            s = jnp.where(dead, -jnp.inf, s)
        m = jnp.max(s, axis=-1, keepdims=True)
        e = jnp.exp(s - m)
        p = e * (1.0 / jnp.sum(e, axis=-1, keepdims=True))
        o = _dot(p.astype(jnp.bfloat16), vb)
        att = jnp.concatenate(
            [jnp.where(lo_half, o[(2 * g) * CHUNK:(2 * g + 1) * CHUNK, :],
                       o[(2 * g + 1) * CHUNK:(2 * g + 2) * CHUNK, :])
             for g in range(GROUP)], axis=1)
        an = att * _rms_rows(att) * gatt * sga_ref[rows, :]
        cat_ref[rows, 0:D_ATT] = an.astype(jnp.bfloat16)

    conv_rows = 2 * CHUNK if seq_tiled else CHUNK
    for i in range(tile // conv_rows):
        src = i * (conv_rows if seq_tiled else u_stride)
        for lane0 in range(0, D_CONV, LANES):
            acc = _dwconv_block(ubuf, cw_ref, src, conv_rows, lane0)
            for j, a in enumerate(acc):
                dst = i * conv_rows + j * SUBLANES
                h_ref[dst:dst + SUBLANES, lane0:lane0 + LANES] = a
    lng, lnb = lng_ref[...], lnb_ref[...]
    for c in range(n_chunks):
        rows = slice(c * CHUNK, (c + 1) * CHUNK)
        h = h_ref[rows, :]
        hc = h - jnp.mean(h, axis=-1, keepdims=True)
        ln = hc * lax.rsqrt(jnp.mean(hc * hc, axis=-1, keepdims=True) + LN_EPS) * lng + lnb
        hn_ref[rows, :] = _silu(ln).astype(jnp.bfloat16)

    gconv = gconv_ref[...]
    for rb in range(tile // MM_ROWS):
        rows = slice(rb * MM_ROWS, (rb + 1) * MM_ROWS)
        cv = _dot(hn_ref[rows, :], wpw_ref[...])
        cn = cv * _rms_rows(cv) * gconv * sgb_ref[rows, :]
        cat_ref[rows, D_ATT:D_MODEL] = cn.astype(jnp.bfloat16)
        o_ref[rows, :] = _dot(cat_ref[rows, :], wout_ref[...])

    gpost = gpost_ref[...]
    for c in range(n_chunks):
        rows = slice(c * CHUNK, (c + 1) * CHUNK)
        y = o_ref[rows, :]
        o_ref[rows, :] = x_ref[rows, :] + y * _rms_rows(y) * gpost

    if seq_tiled:
        nc_ref[0] = ubuf[tile:tile + LEFT, :]
        kbuf[0:WINDOW, :] = kbuf[tile:tile + WINDOW, :]
        vbuf[0:WINDOW, :] = vbuf[tile:tile + WINDOW, :]
        ubuf[0:LEFT, :] = ubuf[tile:tile + LEFT, :]
    else:
        for j in range(n_chunks):
            nc_ref[j] = ubuf[j * u_stride + CHUNK:j * u_stride + CHUNK + LEFT, :]


def _const_spec(shape):
    return pl.BlockSpec(shape, lambda i, s: (0,) * len(shape))


def _run_layer(x2d, mk, mv, left, caches, weights, *, tile, seq_tiled, n_streams, seq_len):
    n_chunks = tile // CHUNK
    if seq_tiled:
        steps = seq_len // tile
        grid = (n_streams, steps)
        x_map = lambda i, s: (i * steps + s, 0)
        st_block, st_map = 1, (lambda i, s: (i, 0, 0))
        left_spec = pl.BlockSpec((1, LEFT, D_CONV), lambda i, s: (0, 0, 0))
        cache_specs = []
        kv_rows = WINDOW + tile
        u_rows = LEFT + tile
    else:
        assert seq_len == CHUNK
        grid = (n_streams // n_chunks, 1)
        x_map = lambda i, s: (i, 0)
        st_block, st_map = n_chunks, (lambda i, s: (i, 0, 0))
        left_spec = pl.BlockSpec((n_chunks, LEFT, D_CONV), st_map)
        cache_specs = [pl.BlockSpec((n_chunks, WINDOW, KV_DIM), st_map)] * 2
        kv_rows = n_chunks * BAND
        u_rows = n_chunks * (LEFT + CHUNK)

    in_specs = ([pl.BlockSpec((tile, D_MODEL), x_map),
                 _const_spec((N_META, KV_DIM)), _const_spec((N_META, KV_DIM)), left_spec]
                + cache_specs + [_const_spec(w.shape) for w in weights])
    out_shape = (jax.ShapeDtypeStruct(x2d.shape, jnp.float32),
                 jax.ShapeDtypeStruct((n_streams, WINDOW, KV_DIM), jnp.float32),
                 jax.ShapeDtypeStruct((n_streams, WINDOW, KV_DIM), jnp.float32),
                 jax.ShapeDtypeStruct((n_streams, LEFT, D_CONV), jnp.float32))
    out_specs = (pl.BlockSpec((tile, D_MODEL), x_map),
                 pl.BlockSpec((st_block, WINDOW, KV_DIM), st_map),
                 pl.BlockSpec((st_block, WINDOW, KV_DIM), st_map),
                 pl.BlockSpec((st_block, LEFT, D_CONV), st_map))
    scratch = [pltpu.VMEM((tile, D_MODEL), jnp.bfloat16),
               pltpu.VMEM((tile, D_ATT), jnp.float32),
               pltpu.VMEM((tile, D_ATT), jnp.float32),
               pltpu.VMEM((tile, D_CONV), jnp.float32),
               pltpu.VMEM((kv_rows, KV_DIM), jnp.bfloat16),
               pltpu.VMEM((kv_rows, KV_DIM), jnp.bfloat16),
               pltpu.VMEM((u_rows, D_CONV), jnp.float32),
               pltpu.VMEM((tile, D_CONV), jnp.float32),
               pltpu.VMEM((tile, D_CONV), jnp.bfloat16),
               pltpu.VMEM((tile, D_MODEL), jnp.bfloat16)]
    return pl.pallas_call(
        functools.partial(_layer_kernel, tile=tile, seq_tiled=seq_tiled),
        grid=grid, in_specs=in_specs, out_specs=out_specs, out_shape=out_shape,
        scratch_shapes=scratch,
        compiler_params=pltpu.CompilerParams(
            dimension_semantics=("arbitrary", "arbitrary"),
            vmem_limit_bytes=VMEM_LIMIT_BYTES),
        name="layer_prompt" if seq_tiled else "layer_sample",
    )(x2d, mk, mv, left, *caches, *weights)


def kernel(x_prompt, x_sample, cache_k, cache_v, state_conv, meta_tokens, g_pre, w_in, sinks,
           g_att, conv_w, ln_g, ln_b, w_pw, g_conv, w_out, g_post):
    assert w_in.shape == (1, D_MODEL, D_IN) and cache_k.shape[2] == WINDOW
    b_p, s_p, _ = x_prompt.shape
    b_s, s_s, _ = x_sample.shape
    perm = _att_perm()
    row = lambda v: v[0][None, :].astype(jnp.float32)

    w = w_in[0]
    w_in_p = jnp.concatenate(
        [w[:, C_Q:C_K][:, perm], w[:, C_K:C_GA], w[:, C_GA:C_A][:, perm], w[:, C_A:]],
        axis=1).astype(jnp.bfloat16)
    w_out_p = jnp.concatenate([w_out[0][:D_ATT][perm], w_out[0][D_ATT:]], axis=0).astype(jnp.bfloat16)
    weights = (row(g_pre), w_in_p, _bias_table(sinks[0]), row(g_att)[:, perm],
               jnp.repeat(conv_w[0], SUBLANES, axis=0),
               row(ln_g), row(ln_b), w_pw[0].astype(jnp.bfloat16), row(g_conv), w_out_p, row(g_post))

    mk, mv, meta_left = pl.pallas_call(
        _meta_kernel,
        out_shape=(jax.ShapeDtypeStruct((N_META, KV_DIM), jnp.float32),
                   jax.ShapeDtypeStruct((N_META, KV_DIM), jnp.float32),
                   jax.ShapeDtypeStruct((LEFT, D_CONV), jnp.float32)),
        compiler_params=pltpu.CompilerParams(vmem_limit_bytes=VMEM_LIMIT_BYTES),
        name="meta_tokens",
    )(meta_tokens, row(g_pre), w_in_p)

    y_p, nk_p, nv_p, nc_p = _run_layer(
        x_prompt.reshape(b_p * s_p, D_MODEL), mk, mv, meta_left[None], (), weights,
        tile=TILE_ROWS, seq_tiled=True, n_streams=b_p, seq_len=s_p)

    ck = cache_k[0].reshape(b_s, WINDOW, KV_DIM)
    cv = cache_v[0].reshape(b_s, WINDOW, KV_DIM)
    left_s = jnp.pad(state_conv[0], ((0, 0), (LEFT - (CONV_W - 1), 0), (0, 0)))
    y_s, nk_s, nv_s, nc_s = _run_layer(
        x_sample.reshape(b_s * s_s, D_MODEL), mk, mv, left_s, (ck, cv), weights,
        tile=TILE_ROWS, seq_tiled=False, n_streams=b_s, seq_len=s_s)

    kv5 = lambda a, n: a.reshape(1, n, WINDOW, N_KV, HEAD_DIM)
    tail = lambda a: a[None, :, LEFT - (CONV_W - 1):, :]
    return (y_p.reshape(b_p, s_p, D_MODEL), y_s.reshape(b_s, s_s, D_MODEL),
            kv5(nk_p, b_p), kv5(nv_p, b_p), tail(nc_p),
            kv5(nk_s, b_s), kv5(nv_s, b_s), tail(nc_s))
```

```python
import functools

import jax
import jax.numpy as jnp
import numpy as np
from jax import lax
from jax.experimental import pallas as pl
from jax.experimental.pallas import tpu as pltpu

D_MODEL = 1024
CHUNK = 64
N_META = 16
HEAD_DIM = 64
D_ATT = 512
D_CONV = 512
N_KV = 2
GROUP = 4
N_HEADS = 8
KV_DIM = N_KV * HEAD_DIM
WINDOW = 128
BAND = WINDOW + CHUNK
CONV_W = 31
LEFT = 32
RMS_EPS = 1e-6
LN_EPS = 1e-5
ATTN_SCALE = HEAD_DIM ** -0.5
D_IN = 2 * D_ATT + 2 * KV_DIM + 3 * D_CONV

C_Q, C_K, C_V, C_GA, C_A, C_B, C_GB = 0, 512, 640, 768, 1280, 1792, 2304

KEYS = N_META + BAND
KPAD = 256
SINK_COL = KEYS

MM_ROWS = 256
SUBLANES, LANES = 8, 128
TILE_ROWS = 512

VMEM_LIMIT_BYTES = 56 * 1024 * 1024


def _att_perm():
    n = np.arange(D_ATT)
    g, h, d = n // 128, (n // 64) % 2, n % 64
    return h * 256 + g * 64 + d


def _bias_table(sinks):
    r = np.arange(N_HEADS * CHUNK)
    g, h, q = r // 128, (r // 64) % 2, r % 64
    head = h * GROUP + g
    slope = 2.0 ** (-(head + 1.0))
    j = np.arange(BAND)
    dist = np.abs(q[:, None] + WINDOW - j[None, :]).astype(np.float32)
    alibi = jnp.asarray(-slope[:, None].astype(np.float32) * dist)
    sink = sinks.astype(jnp.float32)[head][:, None]
    return jnp.concatenate(
        [jnp.zeros((r.size, N_META), jnp.float32), alibi, sink,
         jnp.full((r.size, KPAD - KEYS - 1), -jnp.inf, jnp.float32)], axis=1)


def _rms_rows(v):
    return lax.rsqrt(jnp.mean(v * v, axis=-1, keepdims=True) + RMS_EPS)


def _silu(v):
    return v * jax.nn.sigmoid(v)


def _dot(a, b):
    return jnp.dot(a, b, preferred_element_type=jnp.float32)


def _dwconv_block(ubuf, cw_ref, src, n_rows, lane0):
    pad = LEFT - (CONV_W - 1)
    n_groups = n_rows // SUBLANES
    lanes = slice(lane0, lane0 + LANES)
    xg = [ubuf[src + SUBLANES * i:src + SUBLANES * (i + 1), lanes]
          for i in range(n_groups + LEFT // SUBLANES)]
    sublane = lax.broadcasted_iota(jnp.int32, (SUBLANES, LANES), 0)
    acc = None
    for r in range(SUBLANES):
        offs = [o for o in range(r, LEFT + 1, SUBLANES) if 0 <= o - pad < CONV_W]
        taps = [(o // SUBLANES, cw_ref[(o - pad) * SUBLANES:(o - pad + 1) * SUBLANES, lanes])
                for o in offs]
        y = []
        for j in range(n_groups + (r > 0)):
            t = None
            for m, w in taps:
                term = xg[j + m] * w
                t = term if t is None else t + term
            y.append(t)
        if r == 0:
            acc = y
        else:
            rolled = [pltpu.roll(v, SUBLANES - r, 0) for v in y]
            keep = sublane < SUBLANES - r
            acc = [a + jnp.where(keep, rolled[j], rolled[j + 1]) for j, a in enumerate(acc)]
    return acc


def _meta_kernel(m_ref, gpre_ref, w_ref, mk_ref, mv_ref, left_ref):
    m = m_ref[...]
    xn = (m * _rms_rows(m) * gpre_ref[...]).astype(jnp.bfloat16)
    kv = _dot(xn, w_ref[:, C_K:C_GA])
    mk_ref[...] = kv[:, :KV_DIM]
    mv_ref[...] = kv[:, KV_DIM:]
    ab = _dot(xn, w_ref[:, C_A:C_GB])
    mu = ab[:, :D_CONV] * jax.nn.sigmoid(ab[:, D_CONV:])
    left_ref[...] = jnp.concatenate(
        [jnp.zeros((LEFT - N_META, D_CONV), jnp.float32), mu], axis=0)


def _layer_kernel(*refs, tile, seq_tiled):
    n_chunks = tile // CHUNK
    if seq_tiled:
        (x_ref, mk_ref, mv_ref, left_ref,
         gpre_ref, win_ref, bias_ref, gatt_ref, cw_ref, lng_ref, lnb_ref, wpw_ref,
         gconv_ref, wout_ref, gpost_ref,
         o_ref, nk_ref, nv_ref, nc_ref,
         xn_ref, q_ref, sga_ref, sgb_ref, kbuf, vbuf, ubuf, h_ref, hn_ref, cat_ref) = refs
        band_stride, u_stride = CHUNK, CHUNK
    else:
        (x_ref, mk_ref, mv_ref, left_ref, ck_ref, cv_ref,
         gpre_ref, win_ref, bias_ref, gatt_ref, cw_ref, lng_ref, lnb_ref, wpw_ref,
         gconv_ref, wout_ref, gpost_ref,
         o_ref, nk_ref, nv_ref, nc_ref,
         xn_ref, q_ref, sga_ref, sgb_ref, kbuf, vbuf, ubuf, h_ref, hn_ref, cat_ref) = refs
        band_stride, u_stride = BAND, LEFT + CHUNK

    first = pl.program_id(1) == 0

    if seq_tiled:
        @pl.when(first)
        def _():
            kbuf[0:WINDOW, :] = jnp.zeros((WINDOW, KV_DIM), jnp.bfloat16)
            vbuf[0:WINDOW, :] = jnp.zeros((WINDOW, KV_DIM), jnp.bfloat16)
            ubuf[0:LEFT, :] = left_ref[0]
    else:
        for j in range(n_chunks):
            kbuf[j * BAND:j * BAND + WINDOW, :] = ck_ref[j].astype(jnp.bfloat16)
            vbuf[j * BAND:j * BAND + WINDOW, :] = cv_ref[j].astype(jnp.bfloat16)
            ubuf[j * u_stride:j * u_stride + LEFT, :] = left_ref[j]

    gpre = gpre_ref[...]
    for c in range(n_chunks):
        rows = slice(c * CHUNK, (c + 1) * CHUNK)
        xc = x_ref[rows, :]
        xn_ref[rows, :] = (xc * _rms_rows(xc) * gpre).astype(jnp.bfloat16)

    for rb in range(tile // MM_ROWS):
        rows = slice(rb * MM_ROWS, (rb + 1) * MM_ROWS)
        xb = xn_ref[rows, :]
        q_ref[rows, :] = _dot(xb, win_ref[:, C_Q:C_K]) * ATTN_SCALE
        kv = _dot(xb, win_ref[:, C_K:C_GA])
        for cc in range(MM_ROWS // CHUNK):
            c = rb * (MM_ROWS // CHUNK) + cc
            piece = kv[cc * CHUNK:(cc + 1) * CHUNK, :]
            dst = WINDOW + c * band_stride
            kbuf[dst:dst + CHUNK, :] = piece[:, :KV_DIM].astype(jnp.bfloat16)
            vbuf[dst:dst + CHUNK, :] = piece[:, KV_DIM:].astype(jnp.bfloat16)
            if not seq_tiled:
                nk_ref[c, 0:WINDOW - CHUNK, :] = ck_ref[c, CHUNK:WINDOW, :]
                nv_ref[c, 0:WINDOW - CHUNK, :] = cv_ref[c, CHUNK:WINDOW, :]
                nk_ref[c, WINDOW - CHUNK:WINDOW, :] = piece[:, :KV_DIM]
                nv_ref[c, WINDOW - CHUNK:WINDOW, :] = piece[:, KV_DIM:]
        if seq_tiled and rb == tile // MM_ROWS - 1:
            nk_ref[0] = kv[MM_ROWS - WINDOW:, :KV_DIM]
            nv_ref[0] = kv[MM_ROWS - WINDOW:, KV_DIM:]
        sga_ref[rows, :] = _silu(_dot(xb, win_ref[:, C_GA:C_A]))
        ab = _dot(xb, win_ref[:, C_A:C_GB])
        u = ab[:, :D_CONV] * jax.nn.sigmoid(ab[:, D_CONV:])
        for cc in range(MM_ROWS // CHUNK):
            c = rb * (MM_ROWS // CHUNK) + cc
            dst = LEFT + c * u_stride
            ubuf[dst:dst + CHUNK, :] = u[cc * CHUNK:(cc + 1) * CHUNK, :]
        sgb_ref[rows, :] = _silu(_dot(xb, win_ref[:, C_GB:D_IN]))

    lane = lax.broadcasted_iota(jnp.int32, (CHUNK, KV_DIM), 1)
    lo_half = lane < HEAD_DIM
    col = lax.broadcasted_iota(jnp.int32, (N_HEADS * CHUNK, KPAD), 1)
    mk = mk_ref[...].astype(jnp.bfloat16)
    mv = mv_ref[...].astype(jnp.bfloat16)
    kv_pad = jnp.zeros((KPAD - KEYS, KV_DIM), jnp.bfloat16)
    gatt = gatt_ref[...]
    for c in range(n_chunks):
        rows = slice(c * CHUNK, (c + 1) * CHUNK)
        qc = q_ref[rows, :]
        blocks = []
        for g in range(GROUP):
            qg = qc[:, g * KV_DIM:(g + 1) * KV_DIM]
            blocks.append(jnp.where(lo_half, qg, 0.0))
            blocks.append(jnp.where(lo_half, 0.0, qg))
        qs = jnp.concatenate(blocks, axis=0).astype(jnp.bfloat16)
        b0 = c * band_stride
        kb = jnp.concatenate([mk, kbuf[b0:b0 + BAND, :], kv_pad], axis=0)
        vb = jnp.concatenate([mv, vbuf[b0:b0 + BAND, :], kv_pad], axis=0)
        s = lax.dot_general(qs, kb, (((1,), (1,)), ((), ())),
                            preferred_element_type=jnp.float32)
        s = s + bias_ref[...]
        if seq_tiled and c < WINDOW // CHUNK:
            n_dead = jnp.where(first, WINDOW - c * CHUNK, 0)
            dead = (col >= N_META) & (col < N_META + n_dead)
            s = jnp.where(dead, -jnp.inf, s)
        m = jnp.max(s, axis=-1, keepdims=True)
        e = jnp.exp(s - m)
        p = e * (1.0 / jnp.sum(e, axis=-1, keepdims=True))
        o = _dot(p.astype(jnp.bfloat16), vb)
        att = jnp.concatenate(
            [jnp.where(lo_half, o[(2 * g) * CHUNK:(2 * g + 1) * CHUNK, :],
                       o[(2 * g + 1) * CHUNK:(2 * g + 2) * CHUNK, :])
             for g in range(GROUP)], axis=1)
        an = att * _rms_rows(att) * gatt * sga_ref[rows, :]
        cat_ref[rows, 0:D_ATT] = an.astype(jnp.bfloat16)

    conv_rows = 2 * CHUNK if seq_tiled else CHUNK
    for i in range(tile // conv_rows):
        src = i * (conv_rows if seq_tiled else u_stride)
        for lane0 in range(0, D_CONV, LANES):
            acc = _dwconv_block(ubuf, cw_ref, src, conv_rows, lane0)
            for j, a in enumerate(acc):
                dst = i * conv_rows + j * SUBLANES
                h_ref[dst:dst + SUBLANES, lane0:lane0 + LANES] = a
    lng, lnb = lng_ref[...], lnb_ref[...]
    for c in range(n_chunks):
        rows = slice(c * CHUNK, (c + 1) * CHUNK)
        h = h_ref[rows, :]
        hc = h - jnp.mean(h, axis=-1, keepdims=True)
        ln = hc * lax.rsqrt(jnp.mean(hc * hc, axis=-1, keepdims=True) + LN_EPS) * lng + lnb
        hn_ref[rows, :] = _silu(ln).astype(jnp.bfloat16)

    gconv = gconv_ref[...]
    for rb in range(tile // MM_ROWS):
        rows = slice(rb * MM_ROWS, (rb + 1) * MM_ROWS)
        cv = _dot(hn_ref[rows, :], wpw_ref[...])
        cn = cv * _rms_rows(cv) * gconv * sgb_ref[rows, :]
        cat_ref[rows, D_ATT:D_MODEL] = cn.astype(jnp.bfloat16)
        o_ref[rows, :] = _dot(cat_ref[rows, :], wout_ref[...])

    gpost = gpost_ref[...]
    for c in range(n_chunks):
        rows = slice(c * CHUNK, (c + 1) * CHUNK)
        y = o_ref[rows, :]
        o_ref[rows, :] = x_ref[rows, :] + y * _rms_rows(y) * gpost

    if seq_tiled:
        nc_ref[0] = ubuf[tile:tile + LEFT, :]
        kbuf[0:WINDOW, :] = kbuf[tile:tile + WINDOW, :]
        vbuf[0:WINDOW, :] = vbuf[tile:tile + WINDOW, :]
        ubuf[0:LEFT, :] = ubuf[tile:tile + LEFT, :]
    else:
        for j in range(n_chunks):
            nc_ref[j] = ubuf[j * u_stride + CHUNK:j * u_stride + CHUNK + LEFT, :]


def _const_spec(shape):
    return pl.BlockSpec(shape, lambda i, s: (0,) * len(shape))


def _run_layer(x2d, mk, mv, left, caches, weights, *, tile, seq_tiled, n_streams, seq_len):
    n_chunks = tile // CHUNK
    if seq_tiled:
        steps = seq_len // tile
        grid = (n_streams, steps)
        x_map = lambda i, s: (i * steps + s, 0)
        st_block, st_map = 1, (lambda i, s: (i, 0, 0))
        left_spec = pl.BlockSpec((1, LEFT, D_CONV), lambda i, s: (0, 0, 0))
        cache_specs = []
        kv_rows = WINDOW + tile
        u_rows = LEFT + tile
    else:
        assert seq_len == CHUNK
        grid = (n_streams // n_chunks, 1)
        x_map = lambda i, s: (i, 0)
        st_block, st_map = n_chunks, (lambda i, s: (i, 0, 0))
        left_spec = pl.BlockSpec((n_chunks, LEFT, D_CONV), st_map)
        cache_specs = [pl.BlockSpec((n_chunks, WINDOW, KV_DIM), st_map)] * 2
        kv_rows = n_chunks * BAND
        u_rows = n_chunks * (LEFT + CHUNK)

    in_specs = ([pl.BlockSpec((tile, D_MODEL), x_map),
                 _const_spec((N_META, KV_DIM)), _const_spec((N_META, KV_DIM)), left_spec]
                + cache_specs + [_const_spec(w.shape) for w in weights])
    out_shape = (jax.ShapeDtypeStruct(x2d.shape, jnp.float32),
                 jax.ShapeDtypeStruct((n_streams, WINDOW, KV_DIM), jnp.float32),
                 jax.ShapeDtypeStruct((n_streams, WINDOW, KV_DIM), jnp.float32),
                 jax.ShapeDtypeStruct((n_streams, LEFT, D_CONV), jnp.float32))
    out_specs = (pl.BlockSpec((tile, D_MODEL), x_map),
                 pl.BlockSpec((st_block, WINDOW, KV_DIM), st_map),
                 pl.BlockSpec((st_block, WINDOW, KV_DIM), st_map),
                 pl.BlockSpec((st_block, LEFT, D_CONV), st_map))
    scratch = [pltpu.VMEM((tile, D_MODEL), jnp.bfloat16),
               pltpu.VMEM((tile, D_ATT), jnp.float32),
               pltpu.VMEM((tile, D_ATT), jnp.float32),
               pltpu.VMEM((tile, D_CONV), jnp.float32),
               pltpu.VMEM((kv_rows, KV_DIM), jnp.bfloat16),
               pltpu.VMEM((kv_rows, KV_DIM), jnp.bfloat16),
               pltpu.VMEM((u_rows, D_CONV), jnp.float32),
               pltpu.VMEM((tile, D_CONV), jnp.float32),
               pltpu.VMEM((tile, D_CONV), jnp.bfloat16),
               pltpu.VMEM((tile, D_MODEL), jnp.bfloat16)]
    return pl.pallas_call(
        functools.partial(_layer_kernel, tile=tile, seq_tiled=seq_tiled),
        grid=grid, in_specs=in_specs, out_specs=out_specs, out_shape=out_shape,
        scratch_shapes=scratch,
        compiler_params=pltpu.CompilerParams(
            dimension_semantics=("arbitrary", "arbitrary"),
            vmem_limit_bytes=VMEM_LIMIT_BYTES),
        name="layer_prompt" if seq_tiled else "layer_sample",
    )(x2d, mk, mv, left, *caches, *weights)


def kernel(x_prompt, x_sample, cache_k, cache_v, state_conv, meta_tokens, g_pre, w_in, sinks,
           g_att, conv_w, ln_g, ln_b, w_pw, g_conv, w_out, g_post):
    assert w_in.shape == (1, D_MODEL, D_IN) and cache_k.shape[2] == WINDOW
    b_p, s_p, _ = x_prompt.shape
    b_s, s_s, _ = x_sample.shape
    perm = _att_perm()
    row = lambda v: v[0][None, :].astype(jnp.float32)

    w = w_in[0]
    w_in_p = jnp.concatenate(
        [w[:, C_Q:C_K][:, perm], w[:, C_K:C_GA], w[:, C_GA:C_A][:, perm], w[:, C_A:]],
        axis=1).astype(jnp.bfloat16)
    w_out_p = jnp.concatenate([w_out[0][:D_ATT][perm], w_out[0][D_ATT:]], axis=0).astype(jnp.bfloat16)
    weights = (row(g_pre), w_in_p, _bias_table(sinks[0]), row(g_att)[:, perm],
               jnp.repeat(conv_w[0], SUBLANES, axis=0),
               row(ln_g), row(ln_b), w_pw[0].astype(jnp.bfloat16), row(g_conv), w_out_p, row(g_post))

    mk, mv, meta_left = pl.pallas_call(
        _meta_kernel,
        out_shape=(jax.ShapeDtypeStruct((N_META, KV_DIM), jnp.float32),
                   jax.ShapeDtypeStruct((N_META, KV_DIM), jnp.float32),
                   jax.ShapeDtypeStruct((LEFT, D_CONV), jnp.float32)),
        compiler_params=pltpu.CompilerParams(vmem_limit_bytes=VMEM_LIMIT_BYTES),
        name="meta_tokens",
    )(meta_tokens, row(g_pre), w_in_p)

    y_p, nk_p, nv_p, nc_p = _run_layer(
        x_prompt.reshape(b_p * s_p, D_MODEL), mk, mv, meta_left[None], (), weights,
        tile=TILE_ROWS, seq_tiled=True, n_streams=b_p, seq_len=s_p)

    ck = cache_k[0].reshape(b_s, WINDOW, KV_DIM)
    cv = cache_v[0].reshape(b_s, WINDOW, KV_DIM)
    left_s = jnp.pad(state_conv[0], ((0, 0), (LEFT - (CONV_W - 1), 0), (0, 0)))
    y_s, nk_s, nv_s, nc_s = _run_layer(
        x_sample.reshape(b_s * s_s, D_MODEL), mk, mv, left_s, (ck, cv), weights,
        tile=TILE_ROWS, seq_tiled=False, n_streams=b_s, seq_len=s_s)

    kv5 = lambda a, n: a.reshape(1, n, WINDOW, N_KV, HEAD_DIM)
    tail = lambda a: a[None, :, LEFT - (CONV_W - 1):, :]
    return (y_p.reshape(b_p, s_p, D_MODEL), y_s.reshape(b_s, s_s, D_MODEL),
            kv5(nk_p, b_p), kv5(nv_p, b_p), tail(nc_p),
            kv5(nk_s, b_s), kv5(nv_s, b_s), tail(nc_s))
```

```python
import functools

import jax
import jax.numpy as jnp
import numpy as np
from jax import lax
from jax.experimental import pallas as pl
from jax.experimental.pallas import tpu as pltpu

D_MODEL = 1024
CHUNK = 64
N_META = 16
HEAD_DIM = 64
D_ATT = 512
D_CONV = 512
N_KV = 2
GROUP = 4
N_HEADS = 8
KV_DIM = N_KV * HEAD_DIM
WINDOW = 128
BAND = WINDOW + CHUNK
CONV_W = 31
CONV_HIST = CONV_W - 1
LEFT = 32
RMS_EPS = 1e-6
LN_EPS = 1e-5
ATTN_SCALE = HEAD_DIM ** -0.5
LOG2E = 1.4426950408889634
D_IN = 2 * D_ATT + 2 * KV_DIM + 3 * D_CONV

C_Q, C_K, C_V, C_GA, C_A, C_B, C_GB = 0, 512, 640, 768, 1280, 1792, 2304

KEYS = N_META + BAND
KPAD = 256

MM_ROWS = 256
SUBLANES, LANES = 8, 128
TILE_ROWS = 512

VMEM_LIMIT_BYTES = 56 * 1024 * 1024


def _att_perm():
    n = np.arange(D_ATT)
    g, h, d = n // 128, (n // 64) % 2, n % 64
    return h * 256 + g * 64 + d


def _in_proj_perm():
    att = _att_perm()
    return np.concatenate([C_Q + att, np.arange(C_K, C_GA), C_GA + att, np.arange(C_A, D_IN)])


def _bias_table(sinks):
    r = np.arange(N_HEADS * CHUNK)
    g, h, q = r // 128, (r // 64) % 2, r % 64
    head = h * GROUP + g
    slope = 2.0 ** (-(head + 1.0))
    dist = np.abs(q[:, None] + WINDOW - np.arange(BAND)[None, :])
    base = np.full((r.size, KPAD), -np.inf, np.float32)
    base[:, :N_META] = 0.0
    base[:, N_META:KEYS] = -slope[:, None] * dist * LOG2E
    is_sink = np.zeros((r.size, KPAD), bool)
    is_sink[:, KEYS] = True
    sink = (sinks.astype(jnp.float32) * LOG2E)[head][:, None]
    return jnp.where(is_sink, sink, base)


def _rms_rows(v):
    return lax.rsqrt(jnp.mean(v * v, axis=-1, keepdims=True) + RMS_EPS)


def _sigmoid(v):
    return 1.0 / (1.0 + jnp.exp2(v * -LOG2E))


def _silu(v):
    h = 0.5 * v
    return h + h * jnp.tanh(h)


def _zero_after(v):
    bits = pltpu.bitcast(v, jnp.int32)
    half = lax.shift_right_logical(bits, jnp.full(bits.shape, 16, jnp.int32))
    return lax.shift_right_logical(half, jnp.full(bits.shape, 16, jnp.int32)).astype(jnp.float32)


def _dot(a, b):
    return jnp.dot(a, b, preferred_element_type=jnp.float32)


def _dwconv_block(ubuf, cw_ref, src, n_rows, lane0):
    pad = LEFT - CONV_HIST
    n_groups = n_rows // SUBLANES
    lanes = slice(lane0, lane0 + LANES)
    xg = [ubuf[src + SUBLANES * i:src + SUBLANES * (i + 1), lanes]
          for i in range(n_groups + LEFT // SUBLANES)]
    sublane = lax.broadcasted_iota(jnp.int32, (SUBLANES, LANES), 0)
    acc = None
    for r in range(SUBLANES):
        offs = [o for o in range(r, LEFT + 1, SUBLANES) if 0 <= o - pad < CONV_W]
        taps = [(o // SUBLANES, cw_ref[(o - pad) * SUBLANES:(o - pad + 1) * SUBLANES, lanes])
                for o in offs]
        y = []
        for j in range(n_groups + (r > 0)):
            t = None
            for m, w in taps:
                term = xg[j + m] * w
                t = term if t is None else t + term
            y.append(t)
        if r == 0:
            acc = y
        else:
            rolled = [pltpu.roll(v, SUBLANES - r, 0) for v in y]
            keep = sublane < SUBLANES - r
            acc = [a + jnp.where(keep, rolled[j], rolled[j + 1]) for j, a in enumerate(acc)]
    return acc


def _meta_kernel(m_ref, gpre_ref, w_ref, mk_ref, mv_ref, left_ref):
    m = m_ref[...]
    xn = (m * _rms_rows(m) * gpre_ref[...]).astype(jnp.bfloat16)
    kv = _dot(xn, w_ref[:, C_K:C_GA])
    mk_ref[...] = kv[:, :KV_DIM]
    mv_ref[...] = kv[:, KV_DIM:]
    ab = _dot(xn, w_ref[:, C_A:C_GB])
    mu = ab[:, :D_CONV] * _sigmoid(ab[:, D_CONV:])
    left_ref[...] = jnp.concatenate(
        [jnp.zeros((LEFT - N_META, D_CONV), jnp.float32), mu], axis=0)


def _layer_kernel(*refs, tile, seq_tiled):
    n_chunks = tile // CHUNK
    if seq_tiled:
        (x_ref, mk_ref, mv_ref, left_ref,
         gpre_ref, win_ref, bias_ref, gatt_ref, cw_ref, lng_ref, lnb_ref, wpw_ref,
         gconv_ref, wout_ref, gpost_ref,
         o_ref, nk_ref, nv_ref, nc_ref,
         xn_ref, q_ref, sga_ref, sgb_ref, kbuf, vbuf, ubuf, h_ref, hn_ref, cat_ref) = refs
        band_stride, u_stride = CHUNK, CHUNK
    else:
        (x_ref, mk_ref, mv_ref, st_ref, ck_ref, cv_ref,
         gpre_ref, win_ref, bias_ref, gatt_ref, cw_ref, lng_ref, lnb_ref, wpw_ref,
         gconv_ref, wout_ref, gpost_ref,
         o_ref, nk_ref, nv_ref, nc_ref,
         xn_ref, q_ref, sga_ref, sgb_ref, kbuf, vbuf, ubuf, h_ref, hn_ref, cat_ref) = refs
        band_stride, u_stride = BAND, LEFT + CHUNK

    first = pl.program_id(1) == 0

    if seq_tiled:
        @pl.when(first)
        def _():
            kbuf[0:WINDOW, :] = jnp.zeros((WINDOW, KV_DIM), jnp.bfloat16)
            vbuf[0:WINDOW, :] = jnp.zeros((WINDOW, KV_DIM), jnp.bfloat16)
            ubuf[0:LEFT, :] = left_ref[...]
    else:
        for j in range(n_chunks):
            kbuf[j * BAND:j * BAND + WINDOW, :] = ck_ref[j].astype(jnp.bfloat16)
            vbuf[j * BAND:j * BAND + WINDOW, :] = cv_ref[j].astype(jnp.bfloat16)
            ubuf[j * u_stride:j * u_stride + SUBLANES, :] = jnp.zeros((SUBLANES, D_CONV), jnp.float32)
            ubuf[j * u_stride + LEFT - CONV_HIST:j * u_stride + LEFT, :] = st_ref[j]

    gpre = gpre_ref[...]
    for c in range(n_chunks):
        rows = slice(c * CHUNK, (c + 1) * CHUNK)
        xc = x_ref[rows, :]
        xn_ref[rows, :] = (xc * _rms_rows(xc) * gpre).astype(jnp.bfloat16)

    def in_proj_steps(rb):
        rows = slice(rb * MM_ROWS, (rb + 1) * MM_ROWS)

        def do_q():
            q_ref[rows, :] = (_dot(xn_ref[rows, :], win_ref[:, C_Q:C_K])
                              * (ATTN_SCALE * LOG2E)).astype(jnp.bfloat16)

        def do_kv():
            kv = _dot(xn_ref[rows, :], win_ref[:, C_K:C_GA])
            for cc in range(MM_ROWS // CHUNK):
                c = rb * (MM_ROWS // CHUNK) + cc
                piece = kv[cc * CHUNK:(cc + 1) * CHUNK, :]
                dst = WINDOW + c * band_stride
                kbuf[dst:dst + CHUNK, :] = piece[:, :KV_DIM].astype(jnp.bfloat16)
                vbuf[dst:dst + CHUNK, :] = piece[:, KV_DIM:].astype(jnp.bfloat16)
                if not seq_tiled:
                    nk_ref[c, 0:WINDOW - CHUNK, :] = ck_ref[c, CHUNK:WINDOW, :]
                    nv_ref[c, 0:WINDOW - CHUNK, :] = cv_ref[c, CHUNK:WINDOW, :]
                    nk_ref[c, WINDOW - CHUNK:WINDOW, :] = piece[:, :KV_DIM]
                    nv_ref[c, WINDOW - CHUNK:WINDOW, :] = piece[:, KV_DIM:]
            if seq_tiled and rb == tile // MM_ROWS - 1:
                nk_ref[0] = kv[MM_ROWS - WINDOW:, :KV_DIM]
                nv_ref[0] = kv[MM_ROWS - WINDOW:, KV_DIM:]

        def do_ga():
            sga_ref[rows, :] = _silu(_dot(xn_ref[rows, :], win_ref[:, C_GA:C_A]))

        def do_ab():
            ab = _dot(xn_ref[rows, :], win_ref[:, C_A:C_GB])
            u = ab[:, :D_CONV] * _sigmoid(ab[:, D_CONV:])
            for cc in range(MM_ROWS // CHUNK):
                c = rb * (MM_ROWS // CHUNK) + cc
                dst = LEFT + c * u_stride
                ubuf[dst:dst + CHUNK, :] = u[cc * CHUNK:(cc + 1) * CHUNK, :]

        def do_gb():
            sgb_ref[rows, :] = _silu(_dot(xn_ref[rows, :], win_ref[:, C_GB:D_IN]))

        return [do_q, do_kv, do_ga, do_ab, do_gb]

    conv_rows = 2 * CHUNK if seq_tiled else CHUNK

    def conv_block(i):
        src = i * (conv_rows if seq_tiled else u_stride)
        done = []
        for lane0 in range(0, D_CONV, LANES):
            acc = _dwconv_block(ubuf, cw_ref, src, conv_rows, lane0)
            for j, a in enumerate(acc):
                dst = i * conv_rows + j * SUBLANES
                h_ref[dst:dst + SUBLANES, lane0:lane0 + LANES] = a
            done.append(_zero_after(functools.reduce(lambda a, b: a + b, acc)))
        return functools.reduce(lambda a, b: a + b, done)

    conv_done = []
    n_blocks = tile // MM_ROWS
    steps = [in_proj_steps(rb) for rb in range(n_blocks)]
    for rb in range(n_blocks):
        steps[rb][3]()
    for rb in range(n_blocks):
        conv_done.append(functools.reduce(lambda a, b: a + b, [
            conv_block(i)
            for i in range(rb * (MM_ROWS // conv_rows), (rb + 1) * (MM_ROWS // conv_rows))]))
    for rb in range(n_blocks):
        for k in (0, 1, 2, 4):
            steps[rb][k]()

    lane = lax.broadcasted_iota(jnp.int32, (CHUNK, KV_DIM), 1)
    lo_half = lane < HEAD_DIM
    keep_lo = jnp.where(lo_half, 1.0, 0.0).astype(jnp.bfloat16)
    keep_hi = jnp.where(lo_half, 0.0, 1.0).astype(jnp.bfloat16)
    col = lax.broadcasted_iota(jnp.int32, (N_HEADS * CHUNK, KPAD), 1)
    mks = [(mk_ref[...] + jnp.concatenate([gate, gate], axis=0)).astype(jnp.bfloat16)
           for gate in conv_done]
    mv = mv_ref[...].astype(jnp.bfloat16)
    kv_pad = jnp.zeros((KPAD - KEYS, KV_DIM), jnp.bfloat16)
    gatt = gatt_ref[...]
    def scores(c):
        qc = q_ref[c * CHUNK:(c + 1) * CHUNK, :]
        heads = []
        for g in range(GROUP):
            qg = qc[:, g * KV_DIM:(g + 1) * KV_DIM]
            heads += [qg * keep_lo, qg * keep_hi]
        qs = jnp.concatenate(heads, axis=0)
        b0 = c * band_stride
        kb = jnp.concatenate([mks[c // (MM_ROWS // CHUNK)], kbuf[b0:b0 + BAND, :], kv_pad],
                             axis=0)
        s = lax.dot_general(qs, kb, (((1,), (1,)), ((), ())),
                            preferred_element_type=jnp.float32)
        s = s + bias_ref[...]
        if seq_tiled and c < WINDOW // CHUNK:
            n_dead = jnp.where(first, WINDOW - c * CHUNK, 0)
            dead = (col >= N_META) & (col < N_META + n_dead)
            s = jnp.where(dead, -jnp.inf, s)
        return s

    for c in range(n_chunks):
        rows = slice(c * CHUNK, (c + 1) * CHUNK)
        s = scores(c)
        b0 = c * band_stride
        vb = jnp.concatenate([mv, vbuf[b0:b0 + BAND, :], kv_pad], axis=0)
        e = jnp.exp2(s - jnp.max(s, axis=-1, keepdims=True))
        o = _dot(e.astype(jnp.bfloat16), vb) * (1.0 / jnp.sum(e, axis=-1, keepdims=True))
        att = jnp.concatenate(
            [jnp.where(lo_half, o[(2 * g) * CHUNK:(2 * g + 1) * CHUNK, :],
                       o[(2 * g + 1) * CHUNK:(2 * g + 2) * CHUNK, :])
             for g in range(GROUP)], axis=1)
        an = att * _rms_rows(att) * gatt * sga_ref[rows, :]
        cat_ref[rows, 0:D_ATT] = an.astype(jnp.bfloat16)

    lng, lnb = lng_ref[...], lnb_ref[...]
    for c in range(n_chunks):
        rows = slice(c * CHUNK, (c + 1) * CHUNK)
        h = h_ref[rows, :]
        hc = h - jnp.mean(h, axis=-1, keepdims=True)
        ln = hc * lax.rsqrt(jnp.mean(hc * hc, axis=-1, keepdims=True) + LN_EPS) * lng + lnb
        hn_ref[rows, :] = _silu(ln).astype(jnp.bfloat16)

    gconv = gconv_ref[...]
    for rb in range(tile // MM_ROWS):
        rows = slice(rb * MM_ROWS, (rb + 1) * MM_ROWS)
        cv = _dot(hn_ref[rows, :], wpw_ref[...])
        cn = cv * _rms_rows(cv) * gconv * sgb_ref[rows, :]
        cat_ref[rows, D_ATT:D_MODEL] = cn.astype(jnp.bfloat16)
        o_ref[rows, :] = _dot(cat_ref[rows, :], wout_ref[...])

    gpost = gpost_ref[...]
    for c in range(n_chunks):
        rows = slice(c * CHUNK, (c + 1) * CHUNK)
        y = o_ref[rows, :]
        o_ref[rows, :] = x_ref[rows, :] + y * _rms_rows(y) * gpost

    if seq_tiled:
        nc_ref[0] = ubuf[tile + LEFT - CONV_HIST:tile + LEFT, :]
        kbuf[0:WINDOW, :] = kbuf[tile:tile + WINDOW, :]
        vbuf[0:WINDOW, :] = vbuf[tile:tile + WINDOW, :]
        ubuf[0:LEFT, :] = ubuf[tile:tile + LEFT, :]
    else:
        for j in range(n_chunks):
            end = (j + 1) * u_stride
            nc_ref[j] = ubuf[end - CONV_HIST:end, :]


def _const_spec(shape):
    return pl.BlockSpec(shape, lambda i, s: (0,) * len(shape))


def _run_layer(x2d, mk, mv, state, caches, weights, *, tile, seq_tiled, n_streams, seq_len):
    n_chunks = tile // CHUNK
    if seq_tiled:
        steps = seq_len // tile
        grid = (n_streams, steps)
        x_map = lambda i, s: (i * steps + s, 0)
        st_block, st_map = 1, (lambda i, s: (i, 0, 0))
        state_spec = _const_spec((LEFT, D_CONV))
        cache_specs = []
        kv_rows = WINDOW + tile
        u_rows = LEFT + tile
    else:
        assert seq_len == CHUNK
        grid = (n_streams // n_chunks, 1)
        x_map = lambda i, s: (i, 0)
        st_block, st_map = n_chunks, (lambda i, s: (i, 0, 0))
        state_spec = pl.BlockSpec((n_chunks, CONV_HIST, D_CONV), st_map)
        cache_specs = [pl.BlockSpec((n_chunks, WINDOW, KV_DIM), st_map)] * 2
        kv_rows = n_chunks * BAND
        u_rows = n_chunks * (LEFT + CHUNK)

    in_specs = ([pl.BlockSpec((tile, D_MODEL), x_map),
                 _const_spec((N_META, KV_DIM)), _const_spec((N_META, KV_DIM)), state_spec]
                + cache_specs + [_const_spec(w.shape) for w in weights])
    out_shape = (jax.ShapeDtypeStruct(x2d.shape, jnp.float32),
                 jax.ShapeDtypeStruct((n_streams, WINDOW, KV_DIM), jnp.float32),
                 jax.ShapeDtypeStruct((n_streams, WINDOW, KV_DIM), jnp.float32),
                 jax.ShapeDtypeStruct((n_streams, CONV_HIST, D_CONV), jnp.float32))
    out_specs = (pl.BlockSpec((tile, D_MODEL), x_map),
                 pl.BlockSpec((st_block, WINDOW, KV_DIM), st_map),
                 pl.BlockSpec((st_block, WINDOW, KV_DIM), st_map),
                 pl.BlockSpec((st_block, CONV_HIST, D_CONV), st_map))
    scratch = [pltpu.VMEM((tile, D_MODEL), jnp.bfloat16),
               pltpu.VMEM((tile, D_ATT), jnp.bfloat16),
               pltpu.VMEM((tile, D_ATT), jnp.float32),
               pltpu.VMEM((tile, D_CONV), jnp.float32),
               pltpu.VMEM((kv_rows, KV_DIM), jnp.bfloat16),
               pltpu.VMEM((kv_rows, KV_DIM), jnp.bfloat16),
               pltpu.VMEM((u_rows, D_CONV), jnp.float32),
               pltpu.VMEM((tile, D_CONV), jnp.float32),
               pltpu.VMEM((tile, D_CONV), jnp.bfloat16),
               pltpu.VMEM((tile, D_MODEL), jnp.bfloat16)]
    return pl.pallas_call(
        functools.partial(_layer_kernel, tile=tile, seq_tiled=seq_tiled),
        grid=grid, in_specs=in_specs, out_specs=out_specs, out_shape=out_shape,
        scratch_shapes=scratch,
        compiler_params=pltpu.CompilerParams(
            dimension_semantics=("arbitrary", "arbitrary"),
            vmem_limit_bytes=VMEM_LIMIT_BYTES),
        name="layer_prompt" if seq_tiled else "layer_sample",
    )(x2d, mk, mv, state, *caches, *weights)


def kernel(x_prompt, x_sample, cache_k, cache_v, state_conv, meta_tokens, g_pre, w_in, sinks,
           g_att, conv_w, ln_g, ln_b, w_pw, g_conv, w_out, g_post):
    assert w_in.shape == (1, D_MODEL, D_IN) and cache_k.shape[2] == WINDOW
    b_p, s_p, _ = x_prompt.shape
    b_s, s_s, _ = x_sample.shape
    att_perm = _att_perm()

    w_in_p = w_in[0][:, _in_proj_perm()].astype(jnp.bfloat16)
    w_out_p = w_out[0][np.concatenate([att_perm, np.arange(D_ATT, D_MODEL)])].astype(jnp.bfloat16)
    weights = (g_pre, w_in_p, _bias_table(sinks[0]), g_att[:, att_perm],
               jnp.repeat(conv_w[0], SUBLANES, axis=0),
               ln_g, ln_b, w_pw[0].astype(jnp.bfloat16), g_conv, w_out_p, g_post)

    mk, mv, meta_left = pl.pallas_call(
        _meta_kernel,
        out_shape=(jax.ShapeDtypeStruct((N_META, KV_DIM), jnp.float32),
                   jax.ShapeDtypeStruct((N_META, KV_DIM), jnp.float32),
                   jax.ShapeDtypeStruct((LEFT, D_CONV), jnp.float32)),
        compiler_params=pltpu.CompilerParams(vmem_limit_bytes=VMEM_LIMIT_BYTES),
        name="meta_tokens",
    )(meta_tokens, g_pre, w_in_p)

    y_p, nk_p, nv_p, nc_p = _run_layer(
        x_prompt.reshape(b_p * s_p, D_MODEL), mk, mv, meta_left, (), weights,
        tile=TILE_ROWS, seq_tiled=True, n_streams=b_p, seq_len=s_p)

    ck = cache_k[0].reshape(b_s, WINDOW, KV_DIM)
    cv = cache_v[0].reshape(b_s, WINDOW, KV_DIM)
    y_s, nk_s, nv_s, nc_s = _run_layer(
        x_sample.reshape(b_s * s_s, D_MODEL), mk, mv, state_conv[0], (ck, cv), weights,
        tile=TILE_ROWS, seq_tiled=False, n_streams=b_s, seq_len=s_s)

    kv5 = lambda a, n: a.reshape(1, n, WINDOW, N_KV, HEAD_DIM)
    return (y_p.reshape(b_p, s_p, D_MODEL), y_s.reshape(b_s, s_s, D_MODEL),
            kv5(nk_p, b_p), kv5(nv_p, b_p), nc_p[None],
            kv5(nk_s, b_s), kv5(nv_s, b_s), nc_s[None])
```

```python
import functools

import jax
import jax.numpy as jnp
import numpy as np
from jax import lax
from jax.experimental import pallas as pl
from jax.experimental.pallas import tpu as pltpu

D_MODEL = 1024
CHUNK = 64
N_META = 16
HEAD_DIM = 64
D_ATT = 512
D_CONV = 512
N_KV = 2
GROUP = 4
N_HEADS = 8
KV_DIM = N_KV * HEAD_DIM
WINDOW = 128
BAND = WINDOW + CHUNK
CONV_W = 31
CONV_HIST = CONV_W - 1
LEFT = 32
RMS_EPS = 1e-6
LN_EPS = 1e-5
ATTN_SCALE = HEAD_DIM ** -0.5
LOG2E = 1.4426950408889634
D_IN = 2 * D_ATT + 2 * KV_DIM + 3 * D_CONV

C_Q, C_K, C_V, C_GA, C_A, C_B, C_GB = 0, 512, 640, 768, 1280, 1792, 2304

KEYS = N_META + BAND
KPAD = 256

MM_ROWS = 256
SUBLANES, LANES = 8, 128
TILE_ROWS = 512

VMEM_LIMIT_BYTES = 56 * 1024 * 1024


def _att_perm():
    n = np.arange(D_ATT)
    g, h, d = n // 128, (n // 64) % 2, n % 64
    return h * 256 + g * 64 + d


def _alibi_table():
    r = np.arange(N_HEADS * CHUNK)
    g, h, q = r // 128, (r // 64) % 2, r % 64
    slope = 2.0 ** (-(h * GROUP + g + 1.0))
    dist = np.abs(q[:, None] + WINDOW - np.arange(BAND)[None, :])
    base = np.full((r.size, KPAD), -np.inf, np.float32)
    base[:, :N_META] = 0.0
    base[:, N_META:KEYS] = -slope[:, None] * dist * LOG2E
    base[:, KEYS] = 0.0
    return base


def _rms_rows(v):
    return lax.rsqrt(jnp.mean(v * v, axis=-1, keepdims=True) + RMS_EPS)


def _sigmoid(v):
    return 1.0 / (1.0 + jnp.exp2(v * -LOG2E))


def _silu(v):
    h = 0.5 * v
    return h + h * jnp.tanh(h)


def _zero_after(v):
    bits = pltpu.bitcast(v, jnp.int32)
    half = lax.shift_right_logical(bits, jnp.full(bits.shape, 16, jnp.int32))
    return lax.shift_right_logical(half, jnp.full(bits.shape, 16, jnp.int32)).astype(jnp.float32)


def _dot(a, b):
    return jnp.dot(a, b, preferred_element_type=jnp.float32)


def _dwconv_block(ubuf, cw_ref, src, n_rows, lane0):
    pad = LEFT - CONV_HIST
    n_groups = n_rows // SUBLANES
    lanes = slice(lane0, lane0 + LANES)
    xg = [ubuf[src + SUBLANES * i:src + SUBLANES * (i + 1), lanes]
          for i in range(n_groups + LEFT // SUBLANES)]
    sublane = lax.broadcasted_iota(jnp.int32, (SUBLANES, LANES), 0)
    acc = None
    for r in range(SUBLANES):
        offs = [o for o in range(r, LEFT + 1, SUBLANES) if 0 <= o - pad < CONV_W]
        taps = [(o // SUBLANES, cw_ref[(o - pad) * SUBLANES:(o - pad + 1) * SUBLANES, lanes])
                for o in offs]
        y = []
        for j in range(n_groups + (r > 0)):
            t = None
            for m, w in taps:
                term = xg[j + m] * w
                t = term if t is None else t + term
            y.append(t)
        if r == 0:
            acc = y
        else:
            rolled = [pltpu.roll(v, SUBLANES - r, 0) for v in y]
            keep = sublane < SUBLANES - r
            acc = [a + jnp.where(keep, rolled[j], rolled[j + 1]) for j, a in enumerate(acc)]
    return acc


def _regroup_heads(src_ref, dst_ref, rows, c0):
    for g in range(GROUP):
        lo = src_ref[rows, c0 + g * HEAD_DIM:c0 + (g + 1) * HEAD_DIM]
        hi = src_ref[rows, c0 + D_ATT // 2 + g * HEAD_DIM:c0 + D_ATT // 2 + (g + 1) * HEAD_DIM]
        dst_ref[rows, c0 + g * KV_DIM:c0 + (g + 1) * KV_DIM] = jnp.concatenate(
            [lo, hi], axis=1).astype(dst_ref.dtype)


def _prep_kernel(sinks_ref, alibi_ref, win_ref, wout_ref, wpw_ref, gatt_ref, cw_ref, meta_ref,
                 gpre_ref,
                 winp_ref, woutp_ref, wpwp_ref, gattp_ref, cw8_ref, bias_ref,
                 mk_ref, mv_ref, left_ref):
    for r0 in range(0, D_MODEL, MM_ROWS):
        rows = slice(r0, r0 + MM_ROWS)
        _regroup_heads(win_ref, winp_ref, rows, C_Q)
        _regroup_heads(win_ref, winp_ref, rows, C_GA)
        winp_ref[rows, C_K:C_GA] = win_ref[rows, C_K:C_GA].astype(jnp.bfloat16)
        winp_ref[rows, C_A:D_IN] = win_ref[rows, C_A:D_IN].astype(jnp.bfloat16)
    _regroup_heads(gatt_ref, gattp_ref, slice(None), 0)
    for g in range(GROUP):
        for h in range(N_KV):
            src = h * (D_ATT // 2) + g * HEAD_DIM
            dst = g * KV_DIM + h * HEAD_DIM
            woutp_ref[dst:dst + HEAD_DIM, :] = wout_ref[src:src + HEAD_DIM, :].astype(jnp.bfloat16)
    woutp_ref[D_ATT:, :] = wout_ref[D_ATT:, :].astype(jnp.bfloat16)
    wpwp_ref[...] = wpw_ref[...].astype(jnp.bfloat16)
    for k in range(CONV_W):
        cw8_ref[k * SUBLANES:(k + 1) * SUBLANES, :] = jnp.broadcast_to(
            cw_ref[k:k + 1, :], (SUBLANES, D_CONV))
    is_sink = lax.broadcasted_iota(jnp.int32, (CHUNK, KPAD), 1) == KEYS
    for g in range(GROUP):
        for h in range(N_KV):
            rows = slice((g * N_KV + h) * CHUNK, (g * N_KV + h + 1) * CHUNK)
            bias_ref[rows, :] = jnp.where(is_sink, sinks_ref[h * GROUP + g] * LOG2E,
                                          alibi_ref[rows, :])

    m = meta_ref[...]
    xn = (m * _rms_rows(m) * gpre_ref[...]).astype(jnp.bfloat16)
    kv = _dot(xn, winp_ref[:, C_K:C_GA])
    mk_ref[...] = kv[:, :KV_DIM]
    mv_ref[...] = kv[:, KV_DIM:]
    ab = _dot(xn, winp_ref[:, C_A:C_GB])
    mu = ab[:, :D_CONV] * _sigmoid(ab[:, D_CONV:])
    left_ref[...] = jnp.concatenate(
        [jnp.zeros((LEFT - N_META, D_CONV), jnp.float32), mu], axis=0)


def _layer_kernel(*refs, tile, seq_tiled):
    n_chunks = tile // CHUNK
    if seq_tiled:
        (x_ref, mk_ref, mv_ref, left_ref,
         gpre_ref, win_ref, bias_ref, gatt_ref, cw_ref, lng_ref, lnb_ref, wpw_ref,
         gconv_ref, wout_ref, gpost_ref,
         o_ref, nk_ref, nv_ref, nc_ref,
         xn_ref, q_ref, sga_ref, sgb_ref, kbuf, vbuf, ubuf, h_ref, hn_ref, cat_ref) = refs
        band_stride, u_stride = CHUNK, CHUNK
    else:
        (x_ref, mk_ref, mv_ref, st_ref, ck_ref, cv_ref,
         gpre_ref, win_ref, bias_ref, gatt_ref, cw_ref, lng_ref, lnb_ref, wpw_ref,
         gconv_ref, wout_ref, gpost_ref,
         o_ref, nk_ref, nv_ref, nc_ref,
         xn_ref, q_ref, sga_ref, sgb_ref, kbuf, vbuf, ubuf, h_ref, hn_ref, cat_ref) = refs
        band_stride, u_stride = BAND, LEFT + CHUNK

    first = pl.program_id(1) == 0

    if seq_tiled:
        @pl.when(first)
        def _():
            kbuf[0:WINDOW, :] = jnp.zeros((WINDOW, KV_DIM), jnp.bfloat16)
            vbuf[0:WINDOW, :] = jnp.zeros((WINDOW, KV_DIM), jnp.bfloat16)
            ubuf[0:LEFT, :] = left_ref[...]
    else:
        for j in range(n_chunks):
            kbuf[j * BAND:j * BAND + WINDOW, :] = ck_ref[j].astype(jnp.bfloat16)
            vbuf[j * BAND:j * BAND + WINDOW, :] = cv_ref[j].astype(jnp.bfloat16)
            ubuf[j * u_stride:j * u_stride + SUBLANES, :] = jnp.zeros((SUBLANES, D_CONV), jnp.float32)
            ubuf[j * u_stride + LEFT - CONV_HIST:j * u_stride + LEFT, :] = st_ref[j]

    gpre = gpre_ref[...]
    for c in range(n_chunks):
        rows = slice(c * CHUNK, (c + 1) * CHUNK)
        xc = x_ref[rows, :]
        xn_ref[rows, :] = (xc * _rms_rows(xc) * gpre).astype(jnp.bfloat16)

    def in_proj_steps(rb):
        rows = slice(rb * MM_ROWS, (rb + 1) * MM_ROWS)

        def do_q():
            q_ref[rows, :] = (_dot(xn_ref[rows, :], win_ref[:, C_Q:C_K])
                              * (ATTN_SCALE * LOG2E)).astype(jnp.bfloat16)

        def do_kv():
            kv = _dot(xn_ref[rows, :], win_ref[:, C_K:C_GA])
            for cc in range(MM_ROWS // CHUNK):
                c = rb * (MM_ROWS // CHUNK) + cc
                piece = kv[cc * CHUNK:(cc + 1) * CHUNK, :]
                dst = WINDOW + c * band_stride
                kbuf[dst:dst + CHUNK, :] = piece[:, :KV_DIM].astype(jnp.bfloat16)
                vbuf[dst:dst + CHUNK, :] = piece[:, KV_DIM:].astype(jnp.bfloat16)
                if not seq_tiled:
                    nk_ref[c, 0:WINDOW - CHUNK, :] = ck_ref[c, CHUNK:WINDOW, :]
                    nv_ref[c, 0:WINDOW - CHUNK, :] = cv_ref[c, CHUNK:WINDOW, :]
                    nk_ref[c, WINDOW - CHUNK:WINDOW, :] = piece[:, :KV_DIM]
                    nv_ref[c, WINDOW - CHUNK:WINDOW, :] = piece[:, KV_DIM:]
            if seq_tiled and rb == tile // MM_ROWS - 1:
                nk_ref[0] = kv[MM_ROWS - WINDOW:, :KV_DIM]
                nv_ref[0] = kv[MM_ROWS - WINDOW:, KV_DIM:]

        def do_ga():
            sga_ref[rows, :] = _silu(_dot(xn_ref[rows, :], win_ref[:, C_GA:C_A]))

        def do_ab():
            ab = _dot(xn_ref[rows, :], win_ref[:, C_A:C_GB])
            u = ab[:, :D_CONV] * _sigmoid(ab[:, D_CONV:])
            for cc in range(MM_ROWS // CHUNK):
                c = rb * (MM_ROWS // CHUNK) + cc
                dst = LEFT + c * u_stride
                ubuf[dst:dst + CHUNK, :] = u[cc * CHUNK:(cc + 1) * CHUNK, :]

        def do_gb():
            sgb_ref[rows, :] = _silu(_dot(xn_ref[rows, :], win_ref[:, C_GB:D_IN]))

        return [do_q, do_kv, do_ga, do_ab, do_gb]

    conv_rows = 2 * CHUNK if seq_tiled else CHUNK

    def conv_block(i):
        src = i * (conv_rows if seq_tiled else u_stride)
        done = []
        for lane0 in range(0, D_CONV, LANES):
            acc = _dwconv_block(ubuf, cw_ref, src, conv_rows, lane0)
            for j, a in enumerate(acc):
                dst = i * conv_rows + j * SUBLANES
                h_ref[dst:dst + SUBLANES, lane0:lane0 + LANES] = a
            done.append(_zero_after(functools.reduce(lambda a, b: a + b, acc)))
        return functools.reduce(lambda a, b: a + b, done)

    conv_done = []
    n_blocks = tile // MM_ROWS
    steps = [in_proj_steps(rb) for rb in range(n_blocks)]
    for rb in range(n_blocks):
        steps[rb][3]()
    for rb in range(n_blocks):
        conv_done.append(functools.reduce(lambda a, b: a + b, [
            conv_block(i)
            for i in range(rb * (MM_ROWS // conv_rows), (rb + 1) * (MM_ROWS // conv_rows))]))
    for rb in range(n_blocks):
        for k in (0, 1, 2, 4):
            steps[rb][k]()

    lane = lax.broadcasted_iota(jnp.int32, (CHUNK, KV_DIM), 1)
    lo_half = lane < HEAD_DIM
    keep_lo = jnp.where(lo_half, 1.0, 0.0).astype(jnp.bfloat16)
    keep_hi = jnp.where(lo_half, 0.0, 1.0).astype(jnp.bfloat16)
    col = lax.broadcasted_iota(jnp.int32, (N_HEADS * CHUNK, KPAD), 1)
    mks = [(mk_ref[...] + jnp.concatenate([gate, gate], axis=0)).astype(jnp.bfloat16)
           for gate in conv_done]
    mv = mv_ref[...].astype(jnp.bfloat16)
    kv_pad = jnp.zeros((KPAD - KEYS, KV_DIM), jnp.bfloat16)
    gatt = gatt_ref[...]
    def scores(c):
        qc = q_ref[c * CHUNK:(c + 1) * CHUNK, :]
        heads = []
        for g in range(GROUP):
            qg = qc[:, g * KV_DIM:(g + 1) * KV_DIM]
            heads += [qg * keep_lo, qg * keep_hi]
        qs = jnp.concatenate(heads, axis=0)
        b0 = c * band_stride
        kb = jnp.concatenate([mks[c // (MM_ROWS // CHUNK)], kbuf[b0:b0 + BAND, :], kv_pad],
                             axis=0)
        s = lax.dot_general(qs, kb, (((1,), (1,)), ((), ())),
                            preferred_element_type=jnp.float32)
        s = s + bias_ref[...]
        if seq_tiled and c < WINDOW // CHUNK:
            n_dead = jnp.where(first, WINDOW - c * CHUNK, 0)
            dead = (col >= N_META) & (col < N_META + n_dead)
            s = jnp.where(dead, -jnp.inf, s)
        return s

    for c in range(n_chunks):
        rows = slice(c * CHUNK, (c + 1) * CHUNK)
        s = scores(c)
        b0 = c * band_stride
        vb = jnp.concatenate([mv, vbuf[b0:b0 + BAND, :], kv_pad], axis=0)
        e = jnp.exp2(s - jnp.max(s, axis=-1, keepdims=True))
        o = _dot(e.astype(jnp.bfloat16), vb) * (1.0 / jnp.sum(e, axis=-1, keepdims=True))
        att = jnp.concatenate(
            [jnp.where(lo_half, o[(2 * g) * CHUNK:(2 * g + 1) * CHUNK, :],
                       o[(2 * g + 1) * CHUNK:(2 * g + 2) * CHUNK, :])
             for g in range(GROUP)], axis=1)
        an = att * _rms_rows(att) * gatt * sga_ref[rows, :]
        cat_ref[rows, 0:D_ATT] = an.astype(jnp.bfloat16)

    lng, lnb = lng_ref[...], lnb_ref[...]
    for c in range(n_chunks):
        rows = slice(c * CHUNK, (c + 1) * CHUNK)
        h = h_ref[rows, :]
        hc = h - jnp.mean(h, axis=-1, keepdims=True)
        ln = hc * lax.rsqrt(jnp.mean(hc * hc, axis=-1, keepdims=True) + LN_EPS) * lng + lnb
        hn_ref[rows, :] = _silu(ln).astype(jnp.bfloat16)

    gconv = gconv_ref[...]
    for rb in range(tile // MM_ROWS):
        rows = slice(rb * MM_ROWS, (rb + 1) * MM_ROWS)
        cv = _dot(hn_ref[rows, :], wpw_ref[...])
        cn = cv * _rms_rows(cv) * gconv * sgb_ref[rows, :]
        cat_ref[rows, D_ATT:D_MODEL] = cn.astype(jnp.bfloat16)
        o_ref[rows, :] = _dot(cat_ref[rows, :], wout_ref[...])

    gpost = gpost_ref[...]
    for c in range(n_chunks):
        rows = slice(c * CHUNK, (c + 1) * CHUNK)
        y = o_ref[rows, :]
        o_ref[rows, :] = x_ref[rows, :] + y * _rms_rows(y) * gpost

    if seq_tiled:
        nc_ref[0] = ubuf[tile + LEFT - CONV_HIST:tile + LEFT, :]
        kbuf[0:WINDOW, :] = kbuf[tile:tile + WINDOW, :]
        vbuf[0:WINDOW, :] = vbuf[tile:tile + WINDOW, :]
        ubuf[0:LEFT, :] = ubuf[tile:tile + LEFT, :]
    else:
        for j in range(n_chunks):
            end = (j + 1) * u_stride
            nc_ref[j] = ubuf[end - CONV_HIST:end, :]


def _const_spec(shape):
    return pl.BlockSpec(shape, lambda i, s: (0,) * len(shape))


def _run_layer(x2d, mk, mv, state, caches, weights, *, tile, seq_tiled, n_streams, seq_len):
    n_chunks = tile // CHUNK
    if seq_tiled:
        steps = seq_len // tile
        grid = (n_streams, steps)
        x_map = lambda i, s: (i * steps + s, 0)
        st_block, st_map = 1, (lambda i, s: (i, 0, 0))
        state_spec = _const_spec((LEFT, D_CONV))
        cache_specs = []
        kv_rows = WINDOW + tile
        u_rows = LEFT + tile
    else:
        assert seq_len == CHUNK
        grid = (n_streams // n_chunks, 1)
        x_map = lambda i, s: (i, 0)
        st_block, st_map = n_chunks, (lambda i, s: (i, 0, 0))
        state_spec = pl.BlockSpec((n_chunks, CONV_HIST, D_CONV), st_map)
        cache_specs = [pl.BlockSpec((n_chunks, WINDOW, KV_DIM), st_map)] * 2
        kv_rows = n_chunks * BAND
        u_rows = n_chunks * (LEFT + CHUNK)

    in_specs = ([pl.BlockSpec((tile, D_MODEL), x_map),
                 _const_spec((N_META, KV_DIM)), _const_spec((N_META, KV_DIM)), state_spec]
                + cache_specs + [_const_spec(w.shape) for w in weights])
    out_shape = (jax.ShapeDtypeStruct(x2d.shape, jnp.float32),
                 jax.ShapeDtypeStruct((n_streams, WINDOW, KV_DIM), jnp.float32),
                 jax.ShapeDtypeStruct((n_streams, WINDOW, KV_DIM), jnp.float32),
                 jax.ShapeDtypeStruct((n_streams, CONV_HIST, D_CONV), jnp.float32))
    out_specs = (pl.BlockSpec((tile, D_MODEL), x_map),
                 pl.BlockSpec((st_block, WINDOW, KV_DIM), st_map),
                 pl.BlockSpec((st_block, WINDOW, KV_DIM), st_map),
                 pl.BlockSpec((st_block, CONV_HIST, D_CONV), st_map))
    scratch = [pltpu.VMEM((tile, D_MODEL), jnp.bfloat16),
               pltpu.VMEM((tile, D_ATT), jnp.bfloat16),
               pltpu.VMEM((tile, D_ATT), jnp.float32),
               pltpu.VMEM((tile, D_CONV), jnp.float32),
               pltpu.VMEM((kv_rows, KV_DIM), jnp.bfloat16),
               pltpu.VMEM((kv_rows, KV_DIM), jnp.bfloat16),
               pltpu.VMEM((u_rows, D_CONV), jnp.float32),
               pltpu.VMEM((tile, D_CONV), jnp.float32),
               pltpu.VMEM((tile, D_CONV), jnp.bfloat16),
               pltpu.VMEM((tile, D_MODEL), jnp.bfloat16)]
    return pl.pallas_call(
        functools.partial(_layer_kernel, tile=tile, seq_tiled=seq_tiled),
        grid=grid, in_specs=in_specs, out_specs=out_specs, out_shape=out_shape,
        scratch_shapes=scratch,
        compiler_params=pltpu.CompilerParams(
            dimension_semantics=("arbitrary", "arbitrary"),
            vmem_limit_bytes=VMEM_LIMIT_BYTES),
        name="layer_prompt" if seq_tiled else "layer_sample",
    )(x2d, mk, mv, state, *caches, *weights)


def kernel(x_prompt, x_sample, cache_k, cache_v, state_conv, meta_tokens, g_pre, w_in, sinks,
           g_att, conv_w, ln_g, ln_b, w_pw, g_conv, w_out, g_post):
    assert w_in.shape == (1, D_MODEL, D_IN) and cache_k.shape[2] == WINDOW
    b_p, s_p, _ = x_prompt.shape
    b_s, s_s, _ = x_sample.shape

    vmem = pl.BlockSpec(memory_space=pltpu.VMEM)
    w_in_p, w_out_p, w_pw_p, g_att_p, cw8, bias, mk, mv, meta_left = pl.pallas_call(
        _prep_kernel,
        in_specs=[pl.BlockSpec(memory_space=pltpu.SMEM)] + [vmem] * 8,
        out_shape=(jax.ShapeDtypeStruct((D_MODEL, D_IN), jnp.bfloat16),
                   jax.ShapeDtypeStruct((D_MODEL, D_MODEL), jnp.bfloat16),
                   jax.ShapeDtypeStruct((D_CONV, D_CONV), jnp.bfloat16),
                   jax.ShapeDtypeStruct((1, D_ATT), jnp.float32),
                   jax.ShapeDtypeStruct((CONV_W * SUBLANES, D_CONV), jnp.float32),
                   jax.ShapeDtypeStruct((N_HEADS * CHUNK, KPAD), jnp.float32),
                   jax.ShapeDtypeStruct((N_META, KV_DIM), jnp.float32),
                   jax.ShapeDtypeStruct((N_META, KV_DIM), jnp.float32),
                   jax.ShapeDtypeStruct((LEFT, D_CONV), jnp.float32)),
        compiler_params=pltpu.CompilerParams(vmem_limit_bytes=VMEM_LIMIT_BYTES),
        name="prepare_params",
    )(sinks[0], _alibi_table(), w_in[0], w_out[0], w_pw[0], g_att, conv_w[0], meta_tokens, g_pre)
    weights = (g_pre, w_in_p, bias, g_att_p, cw8, ln_g, ln_b, w_pw_p, g_conv, w_out_p, g_post)

    y_p, nk_p, nv_p, nc_p = _run_layer(
        x_prompt.reshape(b_p * s_p, D_MODEL), mk, mv, meta_left, (), weights,
        tile=TILE_ROWS, seq_tiled=True, n_streams=b_p, seq_len=s_p)

    ck = cache_k[0].reshape(b_s, WINDOW, KV_DIM)
    cv = cache_v[0].reshape(b_s, WINDOW, KV_DIM)
    y_s, nk_s, nv_s, nc_s = _run_layer(
        x_sample.reshape(b_s * s_s, D_MODEL), mk, mv, state_conv[0], (ck, cv), weights,
        tile=TILE_ROWS, seq_tiled=False, n_streams=b_s, seq_len=s_s)

    kv5 = lambda a, n: a.reshape(1, n, WINDOW, N_KV, HEAD_DIM)
    return (y_p.reshape(b_p, s_p, D_MODEL), y_s.reshape(b_s, s_s, D_MODEL),
            kv5(nk_p, b_p), kv5(nv_p, b_p), nc_p[None],
            kv5(nk_s, b_s), kv5(nv_s, b_s), nc_s[None])
```

```python
import functools

import jax
import jax.numpy as jnp
import numpy as np
from jax import lax
from jax.experimental import pallas as pl
from jax.experimental.pallas import tpu as pltpu

D_MODEL = 1024
CHUNK = 64
N_META = 16
HEAD_DIM = 64
D_ATT = 512
D_CONV = 512
N_KV = 2
GROUP = 4
N_HEADS = 8
KV_DIM = N_KV * HEAD_DIM
WINDOW = 128
BAND = WINDOW + CHUNK
CONV_W = 31
CONV_HIST = CONV_W - 1
LEFT = 32
RMS_EPS = 1e-6
LN_EPS = 1e-5
ATTN_SCALE = HEAD_DIM ** -0.5
LOG2E = 1.4426950408889634
D_IN = 2 * D_ATT + 2 * KV_DIM + 3 * D_CONV

C_Q, C_K, C_V, C_GA, C_A, C_B, C_GB = 0, 512, 640, 768, 1280, 1792, 2304

KEYS = N_META + BAND
KPAD = 256

MM_ROWS = 256
SUBLANES, LANES = 8, 128
PROMPT_TILE = 1024
SAMPLE_TILE = 512

VMEM_LIMIT_BYTES = 56 * 1024 * 1024


def _alibi_table():
    r = np.arange(N_HEADS * CHUNK)
    g, h, q = r // 128, (r // 64) % 2, r % 64
    slope = 2.0 ** (-(h * GROUP + g + 1.0))
    dist = np.abs(q[:, None] + WINDOW - np.arange(BAND)[None, :])
    base = np.full((r.size, KPAD), -np.inf, np.float32)
    base[:, :N_META] = 0.0
    base[:, N_META:KEYS] = -slope[:, None] * dist * LOG2E
    base[:, KEYS] = 0.0
    return base


def _rms_rows(v):
    return lax.rsqrt(jnp.mean(v * v, axis=-1, keepdims=True) + RMS_EPS)


def _sigmoid(v):
    return 1.0 / (1.0 + jnp.exp2(v * -LOG2E))


def _silu(v):
    h = 0.5 * v
    return h + h * jnp.tanh(h)


def _zero_after(v):
    bits = pltpu.bitcast(v, jnp.int32)
    half = lax.shift_right_logical(bits, jnp.full(bits.shape, 16, jnp.int32))
    return lax.shift_right_logical(half, jnp.full(bits.shape, 16, jnp.int32)).astype(jnp.float32)


def _dot(a, b):
    return jnp.dot(a, b, preferred_element_type=jnp.float32)


def _dwconv_block(ubuf, cw_ref, src, n_rows, lane0):
    pad = LEFT - CONV_HIST
    n_groups = n_rows // SUBLANES
    lanes = slice(lane0, lane0 + LANES)
    xg = [ubuf[src + SUBLANES * i:src + SUBLANES * (i + 1), lanes]
          for i in range(n_groups + LEFT // SUBLANES)]
    sublane = lax.broadcasted_iota(jnp.int32, (SUBLANES, LANES), 0)
    acc = None
    for r in range(SUBLANES):
        offs = [o for o in range(r, LEFT + 1, SUBLANES) if 0 <= o - pad < CONV_W]
        taps = [(o // SUBLANES, cw_ref[(o - pad) * SUBLANES:(o - pad + 1) * SUBLANES, lanes])
                for o in offs]
        y = []
        for j in range(n_groups + (r > 0)):
            t = None
            for m, w in taps:
                term = xg[j + m] * w
                t = term if t is None else t + term
            y.append(t)
        if r == 0:
            acc = y
        else:
            rolled = [pltpu.roll(v, SUBLANES - r, 0) for v in y]
            keep = sublane < SUBLANES - r
            acc = [a + jnp.where(keep, rolled[j], rolled[j + 1]) for j, a in enumerate(acc)]
    return acc


def _regroup_heads(src_ref, dst_ref, rows, c0):
    for g in range(GROUP):
        lo = src_ref[rows, c0 + g * HEAD_DIM:c0 + (g + 1) * HEAD_DIM]
        hi = src_ref[rows, c0 + D_ATT // 2 + g * HEAD_DIM:c0 + D_ATT // 2 + (g + 1) * HEAD_DIM]
        dst_ref[rows, c0 + g * KV_DIM:c0 + (g + 1) * KV_DIM] = jnp.concatenate(
            [lo, hi], axis=1).astype(dst_ref.dtype)


def _prep_kernel(sinks_ref, alibi_ref, win_ref, wout_ref, wpw_ref, gatt_ref, cw_ref, meta_ref,
                 gpre_ref,
                 winp_ref, woutp_ref, wpwp_ref, gattp_ref, cw8_ref, bias_ref,
                 mk_ref, mv_ref, left_ref):
    for r0 in range(0, D_MODEL, MM_ROWS):
        rows = slice(r0, r0 + MM_ROWS)
        _regroup_heads(win_ref, winp_ref, rows, C_Q)
        _regroup_heads(win_ref, winp_ref, rows, C_GA)
        winp_ref[rows, C_K:C_GA] = win_ref[rows, C_K:C_GA].astype(jnp.bfloat16)
        winp_ref[rows, C_A:D_IN] = win_ref[rows, C_A:D_IN].astype(jnp.bfloat16)
    _regroup_heads(gatt_ref, gattp_ref, slice(None), 0)
    for g in range(GROUP):
        for h in range(N_KV):
            src = h * (D_ATT // 2) + g * HEAD_DIM
            dst = g * KV_DIM + h * HEAD_DIM
            woutp_ref[dst:dst + HEAD_DIM, :] = wout_ref[src:src + HEAD_DIM, :].astype(jnp.bfloat16)
    woutp_ref[D_ATT:, :] = wout_ref[D_ATT:, :].astype(jnp.bfloat16)
    wpwp_ref[...] = wpw_ref[...].astype(jnp.bfloat16)
    for k in range(CONV_W):
        cw8_ref[k * SUBLANES:(k + 1) * SUBLANES, :] = jnp.broadcast_to(
            cw_ref[k:k + 1, :], (SUBLANES, D_CONV))
    is_sink = lax.broadcasted_iota(jnp.int32, (CHUNK, KPAD), 1) == KEYS
    for g in range(GROUP):
        for h in range(N_KV):
            rows = slice((g * N_KV + h) * CHUNK, (g * N_KV + h + 1) * CHUNK)
            bias_ref[rows, :] = jnp.where(is_sink, sinks_ref[h * GROUP + g] * LOG2E,
                                          alibi_ref[rows, :])

    m = meta_ref[...]
    xn = (m * _rms_rows(m) * gpre_ref[...]).astype(jnp.bfloat16)
    kv = _dot(xn, winp_ref[:, C_K:C_GA])
    mk_ref[...] = kv[:, :KV_DIM]
    mv_ref[...] = kv[:, KV_DIM:]
    ab = _dot(xn, winp_ref[:, C_A:C_GB])
    mu = ab[:, :D_CONV] * _sigmoid(ab[:, D_CONV:])
    left_ref[...] = jnp.concatenate(
        [jnp.zeros((LEFT - N_META, D_CONV), jnp.float32), mu], axis=0)


def _layer_kernel(*refs, tile, seq_tiled):
    n_chunks = tile // CHUNK
    if seq_tiled:
        (x_ref, mk_ref, mv_ref, left_ref,
         gpre_ref, win_ref, bias_ref, gatt_ref, cw_ref, lng_ref, lnb_ref, wpw_ref,
         gconv_ref, wout_ref, gpost_ref,
         o_ref, nk_ref, nv_ref, nc_ref,
         xn_ref, q_ref, sga_ref, sgb_ref, kbuf, vbuf, ubuf, h_ref, hn_ref, cat_ref) = refs
        band_stride, u_stride = CHUNK, CHUNK
    else:
        (x_ref, mk_ref, mv_ref, st_ref, ck_ref, cv_ref,
         gpre_ref, win_ref, bias_ref, gatt_ref, cw_ref, lng_ref, lnb_ref, wpw_ref,
         gconv_ref, wout_ref, gpost_ref,
         o_ref, nk_ref, nv_ref, nc_ref,
         xn_ref, q_ref, sga_ref, sgb_ref, kbuf, vbuf, ubuf, h_ref, hn_ref, cat_ref) = refs
        band_stride, u_stride = BAND, LEFT + CHUNK

    first = pl.program_id(1) == 0

    if seq_tiled:
        @pl.when(first)
        def _():
            kbuf[0:WINDOW, :] = jnp.zeros((WINDOW, KV_DIM), jnp.bfloat16)
            vbuf[0:WINDOW, :] = jnp.zeros((WINDOW, KV_DIM), jnp.bfloat16)
            ubuf[0:LEFT, :] = left_ref[...]
    else:
        for j in range(n_chunks):
            kbuf[j * BAND:j * BAND + WINDOW, :] = ck_ref[j].astype(jnp.bfloat16)
            vbuf[j * BAND:j * BAND + WINDOW, :] = cv_ref[j].astype(jnp.bfloat16)
            ubuf[j * u_stride:j * u_stride + SUBLANES, :] = jnp.zeros((SUBLANES, D_CONV), jnp.float32)
            ubuf[j * u_stride + LEFT - CONV_HIST:j * u_stride + LEFT, :] = st_ref[j]

    gpre = gpre_ref[...]
    for c in range(n_chunks):
        rows = slice(c * CHUNK, (c + 1) * CHUNK)
        xc = x_ref[rows, :]
        xn_ref[rows, :] = (xc * _rms_rows(xc) * gpre).astype(jnp.bfloat16)

    def in_proj_steps(rb):
        rows = slice(rb * MM_ROWS, (rb + 1) * MM_ROWS)

        def do_q():
            q_ref[rows, :] = (_dot(xn_ref[rows, :], win_ref[:, C_Q:C_K])
                              * (ATTN_SCALE * LOG2E)).astype(jnp.bfloat16)

        def do_kv():
            kv = _dot(xn_ref[rows, :], win_ref[:, C_K:C_GA])
            for cc in range(MM_ROWS // CHUNK):
                c = rb * (MM_ROWS // CHUNK) + cc
                piece = kv[cc * CHUNK:(cc + 1) * CHUNK, :]
                dst = WINDOW + c * band_stride
                kbuf[dst:dst + CHUNK, :] = piece[:, :KV_DIM].astype(jnp.bfloat16)
                vbuf[dst:dst + CHUNK, :] = piece[:, KV_DIM:].astype(jnp.bfloat16)
                if not seq_tiled:
                    nk_ref[c, 0:WINDOW - CHUNK, :] = ck_ref[c, CHUNK:WINDOW, :]
                    nv_ref[c, 0:WINDOW - CHUNK, :] = cv_ref[c, CHUNK:WINDOW, :]
                    nk_ref[c, WINDOW - CHUNK:WINDOW, :] = piece[:, :KV_DIM]
                    nv_ref[c, WINDOW - CHUNK:WINDOW, :] = piece[:, KV_DIM:]
            if seq_tiled and rb == tile // MM_ROWS - 1:
                nk_ref[0] = kv[MM_ROWS - WINDOW:, :KV_DIM]
                nv_ref[0] = kv[MM_ROWS - WINDOW:, KV_DIM:]

        def do_ga():
            sga_ref[rows, :] = _silu(_dot(xn_ref[rows, :], win_ref[:, C_GA:C_A]))

        def do_ab():
            ab = _dot(xn_ref[rows, :], win_ref[:, C_A:C_GB])
            u = ab[:, :D_CONV] * _sigmoid(ab[:, D_CONV:])
            for cc in range(MM_ROWS // CHUNK):
                c = rb * (MM_ROWS // CHUNK) + cc
                dst = LEFT + c * u_stride
                ubuf[dst:dst + CHUNK, :] = u[cc * CHUNK:(cc + 1) * CHUNK, :]

        def do_gb():
            sgb_ref[rows, :] = _silu(_dot(xn_ref[rows, :], win_ref[:, C_GB:D_IN]))

        return [do_q, do_kv, do_ga, do_ab, do_gb]

    conv_rows = 2 * CHUNK if seq_tiled else CHUNK

    def conv_block(i):
        src = i * (conv_rows if seq_tiled else u_stride)
        done = []
        for lt in range(D_CONV // LANES):
            acc = _dwconv_block(ubuf, cw_ref, src, conv_rows, lt * LANES)
            for j, a in enumerate(acc):
                dst = i * conv_rows + j * SUBLANES
                h_ref[dst:dst + SUBLANES, lt * LANES:(lt + 1) * LANES] = a
            done.append(_zero_after(functools.reduce(lambda a, b: a + b, acc)))
        return functools.reduce(lambda a, b: a + b, done)

    conv_done = []
    n_blocks = tile // MM_ROWS
    steps = [in_proj_steps(rb) for rb in range(n_blocks)]
    for rb in range(n_blocks):
        steps[rb][3]()
        conv_done.append(functools.reduce(lambda a, b: a + b, [
            conv_block(i)
            for i in range(rb * (MM_ROWS // conv_rows), (rb + 1) * (MM_ROWS // conv_rows))]))
    for rb in range(n_blocks):
        for k in (0, 1, 2, 4):
            steps[rb][k]()

    lane = lax.broadcasted_iota(jnp.int32, (CHUNK, KV_DIM), 1)
    lo_half = lane < HEAD_DIM
    keep_lo = jnp.where(lo_half, 1.0, 0.0).astype(jnp.bfloat16)
    keep_hi = jnp.where(lo_half, 0.0, 1.0).astype(jnp.bfloat16)
    col = lax.broadcasted_iota(jnp.int32, (N_HEADS * CHUNK, KPAD), 1)
    mks = [(mk_ref[...] + jnp.concatenate([gate, gate], axis=0)).astype(jnp.bfloat16)
           for gate in conv_done]
    mv = mv_ref[...].astype(jnp.bfloat16)
    kv_pad = jnp.zeros((KPAD - KEYS, KV_DIM), jnp.bfloat16)
    gatt = gatt_ref[...]

    def scores(c):
        qc = q_ref[c * CHUNK:(c + 1) * CHUNK, :]
        heads = []
        for g in range(GROUP):
            qg = qc[:, g * KV_DIM:(g + 1) * KV_DIM]
            heads += [qg * keep_lo, qg * keep_hi]
        qs = jnp.concatenate(heads, axis=0)
        b0 = c * band_stride
        kb = jnp.concatenate([mks[c // (MM_ROWS // CHUNK)], kbuf[b0:b0 + BAND, :], kv_pad],
                             axis=0)
        s = lax.dot_general(qs, kb, (((1,), (1,)), ((), ())),
                            preferred_element_type=jnp.float32)
        s = s + bias_ref[...]
        if seq_tiled and c < WINDOW // CHUNK:
            n_dead = jnp.where(first, WINDOW - c * CHUNK, 0)
            dead = (col >= N_META) & (col < N_META + n_dead)
            s = jnp.where(dead, -jnp.inf, s)
        return s

    for c in range(n_chunks):
        rows = slice(c * CHUNK, (c + 1) * CHUNK)
        s = scores(c)
        b0 = c * band_stride
        vb = jnp.concatenate([mv, vbuf[b0:b0 + BAND, :], kv_pad], axis=0)
        e = jnp.exp2(s - jnp.max(s, axis=-1, keepdims=True))
        o = _dot(e.astype(jnp.bfloat16), vb) * (1.0 / jnp.sum(e, axis=-1, keepdims=True))
        att = jnp.concatenate(
            [jnp.where(lo_half, o[(2 * g) * CHUNK:(2 * g + 1) * CHUNK, :],
                       o[(2 * g + 1) * CHUNK:(2 * g + 2) * CHUNK, :])
             for g in range(GROUP)], axis=1)
        an = att * _rms_rows(att) * gatt * sga_ref[rows, :]
        cat_ref[rows, 0:D_ATT] = an.astype(jnp.bfloat16)

    lng, lnb = lng_ref[...], lnb_ref[...]
    for c in range(n_chunks):
        rows = slice(c * CHUNK, (c + 1) * CHUNK)
        h = h_ref[rows, :]
        hc = h - jnp.mean(h, axis=-1, keepdims=True)
        ln = hc * lax.rsqrt(jnp.mean(hc * hc, axis=-1, keepdims=True) + LN_EPS) * lng + lnb
        hn_ref[rows, :] = _silu(ln).astype(jnp.bfloat16)

    gconv = gconv_ref[...]
    for rb in range(tile // MM_ROWS):
        rows = slice(rb * MM_ROWS, (rb + 1) * MM_ROWS)
        cv = _dot(hn_ref[rows, :], wpw_ref[...])
        cn = cv * _rms_rows(cv) * gconv * sgb_ref[rows, :]
        cat_ref[rows, D_ATT:D_MODEL] = cn.astype(jnp.bfloat16)
        o_ref[rows, :] = _dot(cat_ref[rows, :], wout_ref[...])

    gpost = gpost_ref[...]
    for c in range(n_chunks):
        rows = slice(c * CHUNK, (c + 1) * CHUNK)
        y = o_ref[rows, :]
        o_ref[rows, :] = x_ref[rows, :] + y * _rms_rows(y) * gpost

    if seq_tiled:
        nc_ref[0] = ubuf[tile + LEFT - CONV_HIST:tile + LEFT, :]
        kbuf[0:WINDOW, :] = kbuf[tile:tile + WINDOW, :]
        vbuf[0:WINDOW, :] = vbuf[tile:tile + WINDOW, :]
        ubuf[0:LEFT, :] = ubuf[tile:tile + LEFT, :]
    else:
        for j in range(n_chunks):
            end = (j + 1) * u_stride
            nc_ref[j] = ubuf[end - CONV_HIST:end, :]


def _const_spec(shape):
    return pl.BlockSpec(shape, lambda i, s: (0,) * len(shape), pipeline_mode=pl.Buffered(1))


def _run_layer(x2d, mk, mv, state, caches, weights, *, tile, seq_tiled, n_streams, seq_len):
    n_chunks = tile // CHUNK
    if seq_tiled:
        steps = seq_len // tile
        grid = (n_streams, steps)
        x_map = lambda i, s: (i * steps + s, 0)
        st_block, st_map = 1, (lambda i, s: (i, 0, 0))
        state_spec = _const_spec((LEFT, D_CONV))
        cache_specs = []
        kv_rows = WINDOW + tile
        u_rows = LEFT + tile
    else:
        assert seq_len == CHUNK
        grid = (n_streams // n_chunks, 1)
        x_map = lambda i, s: (i, 0)
        st_block, st_map = n_chunks, (lambda i, s: (i, 0, 0))
        state_spec = pl.BlockSpec((n_chunks, CONV_HIST, D_CONV), st_map)
        cache_specs = [pl.BlockSpec((n_chunks, WINDOW, KV_DIM), st_map)] * 2
        kv_rows = n_chunks * BAND
        u_rows = n_chunks * (LEFT + CHUNK)

    in_specs = ([pl.BlockSpec((tile, D_MODEL), x_map),
                 _const_spec((N_META, KV_DIM)), _const_spec((N_META, KV_DIM)), state_spec]
                + cache_specs + [_const_spec(w.shape) for w in weights])
    out_shape = (jax.ShapeDtypeStruct(x2d.shape, jnp.float32),
                 jax.ShapeDtypeStruct((n_streams, WINDOW, KV_DIM), jnp.float32),
                 jax.ShapeDtypeStruct((n_streams, WINDOW, KV_DIM), jnp.float32),
                 jax.ShapeDtypeStruct((n_streams, CONV_HIST, D_CONV), jnp.float32))
    out_specs = (pl.BlockSpec((tile, D_MODEL), x_map),
                 pl.BlockSpec((st_block, WINDOW, KV_DIM), st_map),
                 pl.BlockSpec((st_block, WINDOW, KV_DIM), st_map),
                 pl.BlockSpec((st_block, CONV_HIST, D_CONV), st_map))
    scratch = [pltpu.VMEM((tile, D_MODEL), jnp.bfloat16),
               pltpu.VMEM((tile, D_ATT), jnp.bfloat16),
               pltpu.VMEM((tile, D_ATT), jnp.float32),
               pltpu.VMEM((tile, D_CONV), jnp.float32),
               pltpu.VMEM((kv_rows, KV_DIM), jnp.bfloat16),
               pltpu.VMEM((kv_rows, KV_DIM), jnp.bfloat16),
               pltpu.VMEM((u_rows, D_CONV), jnp.float32),
               pltpu.VMEM((tile, D_CONV), jnp.float32),
               pltpu.VMEM((tile, D_CONV), jnp.bfloat16),
               pltpu.VMEM((tile, D_MODEL), jnp.bfloat16)]
    return pl.pallas_call(
        functools.partial(_layer_kernel, tile=tile, seq_tiled=seq_tiled),
        grid=grid, in_specs=in_specs, out_specs=out_specs, out_shape=out_shape,
        scratch_shapes=scratch,
        compiler_params=pltpu.CompilerParams(
            dimension_semantics=("arbitrary", "arbitrary"),
            vmem_limit_bytes=VMEM_LIMIT_BYTES),
        name="layer_prompt" if seq_tiled else "layer_sample",
    )(x2d, mk, mv, state, *caches, *weights)


def kernel(x_prompt, x_sample, cache_k, cache_v, state_conv, meta_tokens, g_pre, w_in, sinks,
           g_att, conv_w, ln_g, ln_b, w_pw, g_conv, w_out, g_post):
    assert w_in.shape == (1, D_MODEL, D_IN) and cache_k.shape[2] == WINDOW
    b_p, s_p, _ = x_prompt.shape
    b_s, s_s, _ = x_sample.shape

    vmem = pl.BlockSpec(memory_space=pltpu.VMEM)
    w_in_p, w_out_p, w_pw_p, g_att_p, cw8, bias, mk, mv, meta_left = pl.pallas_call(
        _prep_kernel,
        in_specs=[pl.BlockSpec(memory_space=pltpu.SMEM)] + [vmem] * 8,
        out_shape=(jax.ShapeDtypeStruct((D_MODEL, D_IN), jnp.bfloat16),
                   jax.ShapeDtypeStruct((D_MODEL, D_MODEL), jnp.bfloat16),
                   jax.ShapeDtypeStruct((D_CONV, D_CONV), jnp.bfloat16),
                   jax.ShapeDtypeStruct((1, D_ATT), jnp.float32),
                   jax.ShapeDtypeStruct((CONV_W * SUBLANES, D_CONV), jnp.float32),
                   jax.ShapeDtypeStruct((N_HEADS * CHUNK, KPAD), jnp.float32),
                   jax.ShapeDtypeStruct((N_META, KV_DIM), jnp.float32),
                   jax.ShapeDtypeStruct((N_META, KV_DIM), jnp.float32),
                   jax.ShapeDtypeStruct((LEFT, D_CONV), jnp.float32)),
        compiler_params=pltpu.CompilerParams(vmem_limit_bytes=VMEM_LIMIT_BYTES),
        name="prepare_params",
    )(sinks[0], _alibi_table(), w_in[0], w_out[0], w_pw[0], g_att, conv_w[0], meta_tokens, g_pre)
    weights = (g_pre, w_in_p, bias, g_att_p, cw8, ln_g, ln_b, w_pw_p, g_conv, w_out_p, g_post)

    y_p, nk_p, nv_p, nc_p = _run_layer(
        x_prompt.reshape(b_p * s_p, D_MODEL), mk, mv, meta_left, (), weights,
        tile=PROMPT_TILE, seq_tiled=True, n_streams=b_p, seq_len=s_p)

    ck = cache_k[0].reshape(b_s, WINDOW, KV_DIM)
    cv = cache_v[0].reshape(b_s, WINDOW, KV_DIM)
    y_s, nk_s, nv_s, nc_s = _run_layer(
        x_sample.reshape(b_s * s_s, D_MODEL), mk, mv, state_conv[0], (ck, cv), weights,
        tile=SAMPLE_TILE, seq_tiled=False, n_streams=b_s, seq_len=s_s)

    kv5 = lambda a, n: a.reshape(1, n, WINDOW, N_KV, HEAD_DIM)
    return (y_p.reshape(b_p, s_p, D_MODEL), y_s.reshape(b_s, s_s, D_MODEL),
            kv5(nk_p, b_p), kv5(nv_p, b_p), nc_p[None],
            kv5(nk_s, b_s), kv5(nv_s, b_s), nc_s[None])
```

```python
import functools

import jax
import jax.numpy as jnp
import numpy as np
from jax import lax
from jax.experimental import pallas as pl
from jax.experimental.pallas import tpu as pltpu

D_MODEL = 1024
CHUNK = 64
N_META = 16
HEAD_DIM = 64
D_ATT = 512
D_CONV = 512
N_KV = 2
GROUP = 4
N_HEADS = 8
KV_DIM = N_KV * HEAD_DIM
WINDOW = 128
BAND = WINDOW + CHUNK
CONV_W = 31
CONV_HIST = CONV_W - 1
LEFT = 32
RMS_EPS = 1e-6
LN_EPS = 1e-5
ATTN_SCALE = HEAD_DIM ** -0.5
LOG2E = 1.4426950408889634
D_IN = 2 * D_ATT + 2 * KV_DIM + 3 * D_CONV

C_Q, C_K, C_V, C_GA, C_A, C_B, C_GB = 0, 512, 640, 768, 1280, 1792, 2304

KEYS = N_META + BAND
KPAD = 256

MM_ROWS = 256
SUBLANES, LANES = 8, 128
PROMPT_TILE = 1024
SAMPLE_TILE = 512

VMEM_LIMIT_BYTES = 56 * 1024 * 1024


def _alibi_table():
    r = np.arange(N_HEADS * CHUNK)
    g, h, q = r // 128, (r // 64) % 2, r % 64
    slope = 2.0 ** (-(h * GROUP + g + 1.0))
    dist = np.abs(q[:, None] + WINDOW - np.arange(BAND)[None, :])
    base = np.full((r.size, KPAD), -np.inf, np.float32)
    base[:, :N_META] = 0.0
    base[:, N_META:KEYS] = -slope[:, None] * dist * LOG2E
    base[:, KEYS] = 0.0
    return base


def _rms_rows(v):
    return lax.rsqrt(jnp.mean(v * v, axis=-1, keepdims=True) + RMS_EPS)


def _sigmoid(v):
    return 1.0 / (1.0 + jnp.exp2(v * -LOG2E))


def _silu(v):
    h = 0.5 * v
    return h + h * jnp.tanh(h)


def _zero_after(v):
    bits = pltpu.bitcast(v, jnp.int32)
    half = lax.shift_right_logical(bits, jnp.full(bits.shape, 16, jnp.int32))
    return lax.shift_right_logical(half, jnp.full(bits.shape, 16, jnp.int32)).astype(jnp.float32)


def _dot(a, b):
    return jnp.dot(a, b, preferred_element_type=jnp.float32)


def _dwconv_block(ubuf, cw_ref, src, n_rows, lane0):
    pad = LEFT - CONV_HIST
    n_groups = n_rows // SUBLANES
    lanes = slice(lane0, lane0 + LANES)
    xg = [ubuf[src + SUBLANES * i:src + SUBLANES * (i + 1), lanes]
          for i in range(n_groups + LEFT // SUBLANES)]
    sublane = lax.broadcasted_iota(jnp.int32, (SUBLANES, LANES), 0)
    acc = None
    for r in range(SUBLANES):
        offs = [o for o in range(r, LEFT + 1, SUBLANES) if 0 <= o - pad < CONV_W]
        taps = [(o // SUBLANES, cw_ref[(o - pad) * SUBLANES:(o - pad + 1) * SUBLANES, lanes])
                for o in offs]
        y = []
        for j in range(n_groups + (r > 0)):
            t = None
            for m, w in taps:
                term = xg[j + m] * w
                t = term if t is None else t + term
            y.append(t)
        if r == 0:
            acc = y
        else:
            rolled = [pltpu.roll(v, SUBLANES - r, 0) for v in y]
            keep = sublane < SUBLANES - r
            acc = [a + jnp.where(keep, rolled[j], rolled[j + 1]) for j, a in enumerate(acc)]
    return acc


def _regroup_heads(src_ref, dst_ref, rows, c0):
    for g in range(GROUP):
        lo = src_ref[rows, c0 + g * HEAD_DIM:c0 + (g + 1) * HEAD_DIM]
        hi = src_ref[rows, c0 + D_ATT // 2 + g * HEAD_DIM:c0 + D_ATT // 2 + (g + 1) * HEAD_DIM]
        dst_ref[rows, c0 + g * KV_DIM:c0 + (g + 1) * KV_DIM] = jnp.concatenate(
            [lo, hi], axis=1).astype(dst_ref.dtype)


def _prep_kernel(sinks_ref, alibi_ref, win_ref, wout_ref, wpw_ref, gatt_ref, cw_ref, meta_ref,
                 gpre_ref,
                 winp_ref, woutp_ref, wpwp_ref, gattp_ref, cw8_ref, bias_ref,
                 mk_ref, mv_ref, left_ref):
    for r0 in range(0, D_MODEL, MM_ROWS):
        rows = slice(r0, r0 + MM_ROWS)
        _regroup_heads(win_ref, winp_ref, rows, C_Q)
        _regroup_heads(win_ref, winp_ref, rows, C_GA)
        winp_ref[rows, C_K:C_GA] = win_ref[rows, C_K:C_GA].astype(jnp.bfloat16)
        winp_ref[rows, C_A:D_IN] = win_ref[rows, C_A:D_IN].astype(jnp.bfloat16)
    _regroup_heads(gatt_ref, gattp_ref, slice(None), 0)
    for g in range(GROUP):
        for h in range(N_KV):
            src = h * (D_ATT // 2) + g * HEAD_DIM
            dst = g * KV_DIM + h * HEAD_DIM
            woutp_ref[dst:dst + HEAD_DIM, :] = wout_ref[src:src + HEAD_DIM, :].astype(jnp.bfloat16)
    woutp_ref[D_ATT:, :] = wout_ref[D_ATT:, :].astype(jnp.bfloat16)
    wpwp_ref[...] = wpw_ref[...].astype(jnp.bfloat16)
    for k in range(CONV_W):
        cw8_ref[k * SUBLANES:(k + 1) * SUBLANES, :] = jnp.broadcast_to(
            cw_ref[k:k + 1, :], (SUBLANES, D_CONV))
    is_sink = lax.broadcasted_iota(jnp.int32, (CHUNK, KPAD), 1) == KEYS
    for g in range(GROUP):
        for h in range(N_KV):
            rows = slice((g * N_KV + h) * CHUNK, (g * N_KV + h + 1) * CHUNK)
            bias_ref[rows, :] = jnp.where(is_sink, sinks_ref[h * GROUP + g] * LOG2E,
                                          alibi_ref[rows, :])

    m = meta_ref[...]
    xn = (m * _rms_rows(m) * gpre_ref[...]).astype(jnp.bfloat16)
    kv = _dot(xn, winp_ref[:, C_K:C_GA])
    mk_ref[...] = kv[:, :KV_DIM]
    mv_ref[...] = kv[:, KV_DIM:]
    ab = _dot(xn, winp_ref[:, C_A:C_GB])
    mu = ab[:, :D_CONV] * _sigmoid(ab[:, D_CONV:])
    left_ref[...] = jnp.concatenate(
        [jnp.zeros((LEFT - N_META, D_CONV), jnp.float32), mu], axis=0)


def _layer_kernel(*refs, tile, seq_tiled):
    n_chunks = tile // CHUNK
    if seq_tiled:
        (x_ref, mk_ref, mv_ref, left_ref,
         gpre_ref, win_ref, bias_ref, gatt_ref, cw_ref, lng_ref, lnb_ref, wpw_ref,
         gconv_ref, wout_ref, gpost_ref,
         o_ref, nk_ref, nv_ref, nc_ref,
         xn_ref, q_ref, sga_ref, sgb_ref, kbuf, vbuf, ubuf, h_ref, hn_ref, cat_ref) = refs
        band_stride, u_stride = CHUNK, CHUNK
    else:
        (x_ref, mk_ref, mv_ref, st_ref, ck_ref, cv_ref,
         gpre_ref, win_ref, bias_ref, gatt_ref, cw_ref, lng_ref, lnb_ref, wpw_ref,
         gconv_ref, wout_ref, gpost_ref,
         o_ref, nk_ref, nv_ref, nc_ref,
         xn_ref, q_ref, sga_ref, sgb_ref, kbuf, vbuf, ubuf, h_ref, hn_ref, cat_ref) = refs
        band_stride, u_stride = BAND, LEFT + CHUNK

    first = pl.program_id(1) == 0

    if seq_tiled:
        @pl.when(first)
        def _():
            kbuf[0:WINDOW, :] = jnp.zeros((WINDOW, KV_DIM), jnp.bfloat16)
            vbuf[0:WINDOW, :] = jnp.zeros((WINDOW, KV_DIM), jnp.bfloat16)
            ubuf[0:LEFT, :] = left_ref[...]
    else:
        for j in range(n_chunks):
            kbuf[j * BAND:j * BAND + WINDOW, :] = ck_ref[j].T.astype(jnp.bfloat16)
            vbuf[j * BAND:j * BAND + WINDOW, :] = cv_ref[j].T.astype(jnp.bfloat16)
            ubuf[j * u_stride:j * u_stride + SUBLANES, :] = jnp.zeros((SUBLANES, D_CONV), jnp.float32)
            ubuf[j * u_stride + LEFT - CONV_HIST:j * u_stride + LEFT, :] = st_ref[:, j, :]

    gpre = gpre_ref[...]
    for c in range(n_chunks):
        rows = slice(c * CHUNK, (c + 1) * CHUNK)
        xc = x_ref[rows, :]
        xn_ref[rows, :] = (xc * _rms_rows(xc) * gpre).astype(jnp.bfloat16)

    def in_proj_steps(rb):
        rows = slice(rb * MM_ROWS, (rb + 1) * MM_ROWS)

        def do_q():
            q_ref[rows, :] = (_dot(xn_ref[rows, :], win_ref[:, C_Q:C_K])
                              * (ATTN_SCALE * LOG2E)).astype(jnp.bfloat16)

        def do_kv():
            kv = _dot(xn_ref[rows, :], win_ref[:, C_K:C_GA])
            for cc in range(MM_ROWS // CHUNK):
                c = rb * (MM_ROWS // CHUNK) + cc
                piece = kv[cc * CHUNK:(cc + 1) * CHUNK, :]
                dst = WINDOW + c * band_stride
                kbuf[dst:dst + CHUNK, :] = piece[:, :KV_DIM].astype(jnp.bfloat16)
                vbuf[dst:dst + CHUNK, :] = piece[:, KV_DIM:].astype(jnp.bfloat16)
                if not seq_tiled:
                    nk_ref[c] = jnp.concatenate(
                        [ck_ref[c].T[CHUNK:WINDOW, :], piece[:, :KV_DIM]], axis=0).T
                    nv_ref[c] = jnp.concatenate(
                        [cv_ref[c].T[CHUNK:WINDOW, :], piece[:, KV_DIM:]], axis=0).T
            if seq_tiled and rb == tile // MM_ROWS - 1:
                nk_ref[0] = kv[MM_ROWS - WINDOW:, :KV_DIM].T
                nv_ref[0] = kv[MM_ROWS - WINDOW:, KV_DIM:].T

        def do_ga():
            sga_ref[rows, :] = _silu(_dot(xn_ref[rows, :], win_ref[:, C_GA:C_A]))

        def do_ab():
            ab = _dot(xn_ref[rows, :], win_ref[:, C_A:C_GB])
            u = ab[:, :D_CONV] * _sigmoid(ab[:, D_CONV:])
            for cc in range(MM_ROWS // CHUNK):
                c = rb * (MM_ROWS // CHUNK) + cc
                dst = LEFT + c * u_stride
                ubuf[dst:dst + CHUNK, :] = u[cc * CHUNK:(cc + 1) * CHUNK, :]

        def do_gb():
            sgb_ref[rows, :] = _silu(_dot(xn_ref[rows, :], win_ref[:, C_GB:D_IN]))

        return [do_q, do_kv, do_ga, do_ab, do_gb]

    conv_rows = 2 * CHUNK if seq_tiled else CHUNK

    def conv_block(i):
        src = i * (conv_rows if seq_tiled else u_stride)
        done = []
        for lt in range(D_CONV // LANES):
            acc = _dwconv_block(ubuf, cw_ref, src, conv_rows, lt * LANES)
            for j, a in enumerate(acc):
                dst = i * conv_rows + j * SUBLANES
                h_ref[dst:dst + SUBLANES, lt * LANES:(lt + 1) * LANES] = a
            done.append(_zero_after(functools.reduce(lambda a, b: a + b, acc)))
        return functools.reduce(lambda a, b: a + b, done)

    conv_done = []
    n_blocks = tile // MM_ROWS
    steps = [in_proj_steps(rb) for rb in range(n_blocks)]
    for rb in range(n_blocks):
        steps[rb][3]()
        conv_done.append(functools.reduce(lambda a, b: a + b, [
            conv_block(i)
            for i in range(rb * (MM_ROWS // conv_rows), (rb + 1) * (MM_ROWS // conv_rows))]))
    for rb in range(n_blocks):
        for k in (0, 1, 2, 4):
            steps[rb][k]()

    lane = lax.broadcasted_iota(jnp.int32, (CHUNK, KV_DIM), 1)
    lo_half = lane < HEAD_DIM
    keep_lo = jnp.where(lo_half, 1.0, 0.0).astype(jnp.bfloat16)
    keep_hi = jnp.where(lo_half, 0.0, 1.0).astype(jnp.bfloat16)
    col = lax.broadcasted_iota(jnp.int32, (N_HEADS * CHUNK, KPAD), 1)
    mks = [(mk_ref[...] + jnp.concatenate([gate, gate], axis=0)).astype(jnp.bfloat16)
           for gate in conv_done]
    mv = mv_ref[...].astype(jnp.bfloat16)
    kv_pad = jnp.zeros((KPAD - KEYS, KV_DIM), jnp.bfloat16)
    gatt = gatt_ref[...]

    def scores(c):
        qc = q_ref[c * CHUNK:(c + 1) * CHUNK, :]
        heads = []
        for g in range(GROUP):
            qg = qc[:, g * KV_DIM:(g + 1) * KV_DIM]
            heads += [qg * keep_lo, qg * keep_hi]
        qs = jnp.concatenate(heads, axis=0)
        b0 = c * band_stride
        kb = jnp.concatenate([mks[c // (MM_ROWS // CHUNK)], kbuf[b0:b0 + BAND, :], kv_pad],
                             axis=0)
        s = lax.dot_general(qs, kb, (((1,), (1,)), ((), ())),
                            preferred_element_type=jnp.float32)
        s = s + bias_ref[...]
        if seq_tiled and c < WINDOW // CHUNK:
            n_dead = jnp.where(first, WINDOW - c * CHUNK, 0)
            dead = (col >= N_META) & (col < N_META + n_dead)
            s = jnp.where(dead, -jnp.inf, s)
        return s

    for c in range(n_chunks):
        rows = slice(c * CHUNK, (c + 1) * CHUNK)
        s = scores(c)
        b0 = c * band_stride
        vb = jnp.concatenate([mv, vbuf[b0:b0 + BAND, :], kv_pad], axis=0)
        e = jnp.exp2(s - jnp.max(s, axis=-1, keepdims=True))
        o = _dot(e.astype(jnp.bfloat16), vb) * (1.0 / jnp.sum(e, axis=-1, keepdims=True))
        att = jnp.concatenate(
            [jnp.where(lo_half, o[(2 * g) * CHUNK:(2 * g + 1) * CHUNK, :],
                       o[(2 * g + 1) * CHUNK:(2 * g + 2) * CHUNK, :])
             for g in range(GROUP)], axis=1)
        an = att * _rms_rows(att) * gatt * sga_ref[rows, :]
        cat_ref[rows, 0:D_ATT] = an.astype(jnp.bfloat16)

    lng, lnb = lng_ref[...], lnb_ref[...]
    for c in range(n_chunks):
        rows = slice(c * CHUNK, (c + 1) * CHUNK)
        h = h_ref[rows, :]
        hc = h - jnp.mean(h, axis=-1, keepdims=True)
        ln = hc * lax.rsqrt(jnp.mean(hc * hc, axis=-1, keepdims=True) + LN_EPS) * lng + lnb
        hn_ref[rows, :] = _silu(ln).astype(jnp.bfloat16)

    gconv = gconv_ref[...]
    for rb in range(tile // MM_ROWS):
        rows = slice(rb * MM_ROWS, (rb + 1) * MM_ROWS)
        cv = _dot(hn_ref[rows, :], wpw_ref[...])
        cn = cv * _rms_rows(cv) * gconv * sgb_ref[rows, :]
        cat_ref[rows, D_ATT:D_MODEL] = cn.astype(jnp.bfloat16)
        o_ref[rows, :] = _dot(cat_ref[rows, :], wout_ref[...])

    gpost = gpost_ref[...]
    for c in range(n_chunks):
        rows = slice(c * CHUNK, (c + 1) * CHUNK)
        y = o_ref[rows, :]
        o_ref[rows, :] = x_ref[rows, :] + y * _rms_rows(y) * gpost

    if seq_tiled:
        nc_ref[:, pl.ds(pl.program_id(0), 1), :] = (
            ubuf[tile + LEFT - CONV_HIST:tile + LEFT, :][:, None, :])
        kbuf[0:WINDOW, :] = kbuf[tile:tile + WINDOW, :]
        vbuf[0:WINDOW, :] = vbuf[tile:tile + WINDOW, :]
        ubuf[0:LEFT, :] = ubuf[tile:tile + LEFT, :]
    else:
        for j in range(n_chunks):
            end = (j + 1) * u_stride
            nc_ref[:, j, :] = ubuf[end - CONV_HIST:end, :]


def _const_spec(shape):
    return pl.BlockSpec(shape, lambda i, s: (0,) * len(shape), pipeline_mode=pl.Buffered(1))


def _run_layer(x2d, mk, mv, state, caches, weights, *, tile, seq_tiled, n_streams, seq_len):
    n_chunks = tile // CHUNK
    if seq_tiled:
        steps = seq_len // tile
        grid = (n_streams, steps)
        x_map = lambda i, s: (i * steps + s, 0)
        st_block, st_map = 1, (lambda i, s: (i, 0, 0))
        state_spec = _const_spec((LEFT, D_CONV))
        conv_out_spec = pl.BlockSpec((CONV_HIST, n_streams, D_CONV), lambda i, s: (0, 0, 0))
        cache_specs = []
        kv_rows = WINDOW + tile
        u_rows = LEFT + tile
    else:
        assert seq_len == CHUNK
        grid = (n_streams // n_chunks, 1)
        x_map = lambda i, s: (i, 0)
        st_block, st_map = n_chunks, (lambda i, s: (i, 0, 0))
        state_spec = pl.BlockSpec((CONV_HIST, n_chunks, D_CONV), lambda i, s: (0, i, 0))
        conv_out_spec = state_spec
        cache_specs = [pl.BlockSpec((n_chunks, WINDOW, KV_DIM), st_map)] * 2
        kv_rows = n_chunks * BAND
        u_rows = n_chunks * (LEFT + CHUNK)

    in_specs = ([pl.BlockSpec((tile, D_MODEL), x_map),
                 _const_spec((N_META, KV_DIM)), _const_spec((N_META, KV_DIM)), state_spec]
                + cache_specs + [_const_spec(w.shape) for w in weights])
    out_shape = (jax.ShapeDtypeStruct(x2d.shape, jnp.float32),
                 jax.ShapeDtypeStruct((n_streams, WINDOW, KV_DIM), jnp.float32),
                 jax.ShapeDtypeStruct((n_streams, WINDOW, KV_DIM), jnp.float32),
                 jax.ShapeDtypeStruct((CONV_HIST, n_streams, D_CONV), jnp.float32))
    out_specs = (pl.BlockSpec((tile, D_MODEL), x_map),
                 pl.BlockSpec((st_block, WINDOW, KV_DIM), st_map),
                 pl.BlockSpec((st_block, WINDOW, KV_DIM), st_map),
                 conv_out_spec)
    scratch = [pltpu.VMEM((tile, D_MODEL), jnp.bfloat16),
               pltpu.VMEM((tile, D_ATT), jnp.bfloat16),
               pltpu.VMEM((tile, D_ATT), jnp.float32),
               pltpu.VMEM((tile, D_CONV), jnp.float32),
               pltpu.VMEM((kv_rows, KV_DIM), jnp.bfloat16),
               pltpu.VMEM((kv_rows, KV_DIM), jnp.bfloat16),
               pltpu.VMEM((u_rows, D_CONV), jnp.float32),
               pltpu.VMEM((tile, D_CONV), jnp.float32),
               pltpu.VMEM((tile, D_CONV), jnp.bfloat16),
               pltpu.VMEM((tile, D_MODEL), jnp.bfloat16)]
    return pl.pallas_call(
        functools.partial(_layer_kernel, tile=tile, seq_tiled=seq_tiled),
        grid=grid, in_specs=in_specs, out_specs=out_specs, out_shape=out_shape,
        scratch_shapes=scratch,
        compiler_params=pltpu.CompilerParams(
            dimension_semantics=("arbitrary", "arbitrary"),
            vmem_limit_bytes=VMEM_LIMIT_BYTES),
        name="layer_prompt" if seq_tiled else "layer_sample",
    )(x2d, mk, mv, state, *caches, *weights)


def kernel(x_prompt, x_sample, cache_k, cache_v, state_conv, meta_tokens, g_pre, w_in, sinks,
           g_att, conv_w, ln_g, ln_b, w_pw, g_conv, w_out, g_post):
    assert w_in.shape == (1, D_MODEL, D_IN) and cache_k.shape[2] == WINDOW
    b_p, s_p, _ = x_prompt.shape
    b_s, s_s, _ = x_sample.shape

    vmem = pl.BlockSpec(memory_space=pltpu.VMEM)
    w_in_p, w_out_p, w_pw_p, g_att_p, cw8, bias, mk, mv, meta_left = pl.pallas_call(
        _prep_kernel,
        in_specs=[pl.BlockSpec(memory_space=pltpu.SMEM)] + [vmem] * 8,
        out_shape=(jax.ShapeDtypeStruct((D_MODEL, D_IN), jnp.bfloat16),
                   jax.ShapeDtypeStruct((D_MODEL, D_MODEL), jnp.bfloat16),
                   jax.ShapeDtypeStruct((D_CONV, D_CONV), jnp.bfloat16),
                   jax.ShapeDtypeStruct((1, D_ATT), jnp.float32),
                   jax.ShapeDtypeStruct((CONV_W * SUBLANES, D_CONV), jnp.float32),
                   jax.ShapeDtypeStruct((N_HEADS * CHUNK, KPAD), jnp.float32),
                   jax.ShapeDtypeStruct((N_META, KV_DIM), jnp.float32),
                   jax.ShapeDtypeStruct((N_META, KV_DIM), jnp.float32),
                   jax.ShapeDtypeStruct((LEFT, D_CONV), jnp.float32)),
        compiler_params=pltpu.CompilerParams(vmem_limit_bytes=VMEM_LIMIT_BYTES),
        name="prepare_params",
    )(sinks[0], _alibi_table(), w_in[0], w_out[0], w_pw[0], g_att, conv_w[0], meta_tokens, g_pre)
    weights = (g_pre, w_in_p, bias, g_att_p, cw8, ln_g, ln_b, w_pw_p, g_conv, w_out_p, g_post)

    y_p, nk_p, nv_p, nc_p = _run_layer(
        x_prompt.reshape(b_p * s_p, D_MODEL), mk, mv, meta_left, (), weights,
        tile=PROMPT_TILE, seq_tiled=True, n_streams=b_p, seq_len=s_p)

    ck = jnp.swapaxes(cache_k[0].reshape(b_s, WINDOW, KV_DIM), 1, 2)
    cv = jnp.swapaxes(cache_v[0].reshape(b_s, WINDOW, KV_DIM), 1, 2)
    y_s, nk_s, nv_s, nc_s = _run_layer(
        x_sample.reshape(b_s * s_s, D_MODEL), mk, mv, jnp.swapaxes(state_conv[0], 0, 1),
        (ck, cv), weights,
        tile=SAMPLE_TILE, seq_tiled=False, n_streams=b_s, seq_len=s_s)

    kv5 = lambda a, n: jnp.swapaxes(a, 1, 2).reshape(1, n, WINDOW, N_KV, HEAD_DIM)
    return (y_p.reshape(b_p, s_p, D_MODEL), y_s.reshape(b_s, s_s, D_MODEL),
            kv5(nk_p, b_p), kv5(nv_p, b_p), jnp.swapaxes(nc_p, 0, 1)[None],
            kv5(nk_s, b_s), kv5(nv_s, b_s), jnp.swapaxes(nc_s, 0, 1)[None])
```
